```python
import math
import jax, jax.numpy as jnp
from jax import lax
import numpy as np

D_MODEL = 1024
BATCH = 2
SEQ = 8192
DEPTH = 1

D_MIX = D_MODEL
N_HEADS_A = 8
HEAD_DIM_A = 64
DIL_CONFIGS = ((128, 1), (512, 4), (2048, 16))
Q_BLOCK = 128
N_HEADS_B = 8
QK_NOPE = 64
QK_ROPE = 32
V_DIM = 64
Q_LORA = 384
KV_LORA = 256
ROPE_BASE = 10000.0
D_FF = 2816
CONV_WIDTH = 3
EPS = 1e-6
NEG = -1e30

WIDTH_A = N_HEADS_A * HEAD_DIM_A
WIDTH_B = N_HEADS_B * V_DIM
IN_SIZES = (WIDTH_A, WIDTH_A, WIDTH_A, Q_LORA, KV_LORA, QK_ROPE)
D_IN = sum(IN_SIZES)
SPLIT_POINTS = tuple(int(v) for v in np.cumsum(IN_SIZES)[:-1])

kernel_name = "hybrid_dilated_swa_mla_convffn_sandwich"


def _rmsnorm(x, g):
    xf = x.astype(jnp.float32)
    y = xf * lax.rsqrt(jnp.mean(xf * xf, axis=-1, keepdims=True) + EPS)
    return (y * g.astype(jnp.float32)).astype(x.dtype)


def _rope(x, cos, sin):
    xf = x.astype(jnp.float32)
    half = xf.shape[-1] // 2
    x1, x2 = xf[..., :half], xf[..., half:]
    out = jnp.concatenate([x1 * cos - x2 * sin, x2 * cos + x1 * sin], axis=-1)
    return out.astype(x.dtype)


def _dilated_branch(q, k, v, slopes, window, dilation):
    b, h, s, d = q.shape
    r = dilation
    half = window // (2 * dilation)
    L = s // r
    qb = min(Q_BLOCK, L)
    nblk = -(-L // qb)
    Lp = nblk * qb
    slab = qb + 2 * half

    def to_dilated(t):
        return t.reshape(b, h, L, r, d).transpose(0, 1, 3, 2, 4)

    qd, kd, vd = to_dilated(q), to_dilated(k), to_dilated(v)
    qd = jnp.pad(qd, ((0, 0), (0, 0), (0, 0), (0, Lp - L), (0, 0)))
    pad_kv = ((0, 0), (0, 0), (0, 0), (half, half + Lp - L), (0, 0))
    kd = jnp.pad(kd, pad_kv)
    vd = jnp.pad(vd, pad_kv)
    idx = jnp.arange(nblk)[:, None] * qb + jnp.arange(slab)[None, :]
    k_s = kd[:, :, :, idx, :].astype(jnp.float32)
    v_s = vd[:, :, :, idx, :].astype(jnp.float32)
    q_s = qd.reshape(b, h, r, nblk, qb, d).astype(jnp.float32)

    scores = jnp.einsum('bhcnqd,bhcnkd->bhcnqk', q_s, k_s) * (d ** -0.5)
    off = jnp.arange(slab)[None, :] - half - jnp.arange(qb)[:, None]
    key_pos = idx - half
    valid = (jnp.abs(off) <= half)[None, :, :] & ((key_pos >= 0) & (key_pos < L))[:, None, :]
    dist = (jnp.abs(off) * r).astype(jnp.float32)
    alibi = -slopes[:, None, None] * dist[None]
    scores = scores + alibi[None, :, None, None]
    scores = jnp.where(valid[None, None, None], scores, NEG)
    lse = jax.nn.logsumexp(scores, axis=-1)
    p = jnp.exp(scores - lse[..., None])
    o = jnp.einsum('bhcnqk,bhcnkd->bhcnqd', p, v_s)

    o = o.reshape(b, h, r, Lp, d)[:, :, :, :L].transpose(0, 1, 3, 2, 4).reshape(b, h, s, d)
    lse = lse.reshape(b, h, r, Lp)[:, :, :, :L].transpose(0, 1, 3, 2).reshape(b, h, s)
    return o, lse


def _dilated_attention(qa, ka, va):
    b, s, _ = qa.shape
    def heads(t):
        return t.reshape(b, s, N_HEADS_A, HEAD_DIM_A).transpose(0, 2, 1, 3)
    q, k, v = heads(qa), heads(ka), heads(va)
    slopes = jnp.exp2(-8.0 * jnp.arange(1, N_HEADS_A + 1, dtype=jnp.float32) / N_HEADS_A)
    outs, lses = [], []
    for window, dilation in DIL_CONFIGS:
        o, lse = _dilated_branch(q, k, v, slopes, window, dilation)
        outs.append(o)
        lses.append(lse)
    w = jax.nn.softmax(jnp.stack(lses, axis=0), axis=0)
    o = jnp.einsum('gbhs,gbhsd->bhsd', w, jnp.stack(outs, axis=0))
    return o.transpose(0, 2, 1, 3).reshape(b, s, WIDTH_A).astype(qa.dtype)


def _mla(c_q, c_kv, k_r, q_lat_norm, w_uq, kv_lat_norm, w_ukv):
    b, s, _ = c_q.shape
    pos = jnp.arange(s, dtype=jnp.float32)
    inv_freq = jnp.exp(-math.log(ROPE_BASE) * jnp.arange(0, QK_ROPE, 2, dtype=jnp.float32) / QK_ROPE)
    ang = pos[:, None] * inv_freq[None, :]
    cos, sin = jnp.cos(ang), jnp.sin(ang)

    q = (_rmsnorm(c_q, q_lat_norm) @ w_uq).reshape(b, s, N_HEADS_B, QK_NOPE + QK_ROPE)
    q_nope = q[..., :QK_NOPE]
    q_rope = _rope(q[..., QK_NOPE:], cos[:, None, :], sin[:, None, :])
    kv = (_rmsnorm(c_kv, kv_lat_norm) @ w_ukv).reshape(b, s, N_HEADS_B, QK_NOPE + V_DIM)
    k_nope = kv[..., :QK_NOPE].transpose(0, 2, 1, 3).astype(jnp.float32)
    v = kv[..., QK_NOPE:].transpose(0, 2, 1, 3).astype(jnp.float32)
    k_rope = _rope(k_r, cos, sin).astype(jnp.float32)
    scale = (QK_NOPE + QK_ROPE) ** -0.5

    nq = s // Q_BLOCK
    qn_blk = q_nope.reshape(b, nq, Q_BLOCK, N_HEADS_B, QK_NOPE).transpose(1, 0, 3, 2, 4)
    qr_blk = q_rope.reshape(b, nq, Q_BLOCK, N_HEADS_B, QK_ROPE).transpose(1, 0, 3, 2, 4)

    def block(args):
        qn, qr = args
        sc = (jnp.einsum('bhqd,bhkd->bhqk', qn.astype(jnp.float32), k_nope)
              + jnp.einsum('bhqr,bkr->bhqk', qr.astype(jnp.float32), k_rope)) * scale
        p = jax.nn.softmax(sc, axis=-1)
        return jnp.einsum('bhqk,bhkd->bhqd', p, v)

    o = lax.map(block, (qn_blk, qr_blk))
    return o.transpose(1, 0, 3, 2, 4).reshape(b, s, WIDTH_B).astype(c_q.dtype)


def _dwconv(u, w, bias):
    c = u.shape[-1]
    y = lax.conv_general_dilated(
        u, w[:, None, :].astype(u.dtype), window_strides=(1,),
        padding=((CONV_WIDTH // 2, CONV_WIDTH // 2),),
        dimension_numbers=('NWC', 'WIO', 'NWC'), feature_group_count=c)
    return y + bias.astype(u.dtype)


def setup_inputs(seed: int = 0) -> dict:
    key = jax.random.key(seed)
    ks = jax.random.split(key, 17)
    f32 = jnp.float32

    def gain(k, n):
        return 1.0 + 0.1 * jax.random.normal(k, (DEPTH, n), f32)

    def dense(k, fan_in, fan_out):
        return jax.random.normal(k, (DEPTH, fan_in, fan_out), f32) * fan_in ** -0.5

    return {
        "x": jax.random.normal(ks[0], (BATCH, SEQ, D_MODEL), f32),
        "norm_mix_pre": gain(ks[1], D_MODEL),
        "w_in": dense(ks[2], D_MODEL, D_IN),
        "q_lat_norm": gain(ks[3], Q_LORA),
        "w_uq": dense(ks[4], Q_LORA, N_HEADS_B * (QK_NOPE + QK_ROPE)),
        "kv_lat_norm": gain(ks[5], KV_LORA),
        "w_ukv": dense(ks[6], KV_LORA, N_HEADS_B * (QK_NOPE + V_DIM)),
        "out_norm_a": gain(ks[7], WIDTH_A),
        "out_norm_b": gain(ks[8], WIDTH_B),
        "w_o": dense(ks[9], D_MIX, D_MODEL),
        "norm_mix_post": gain(ks[10], D_MODEL),
        "norm_ffn_pre": gain(ks[11], D_MODEL),
        "w_up": dense(ks[12], D_MODEL, 2 * D_FF),
        "conv_w": jax.random.normal(ks[13], (DEPTH, CONV_WIDTH, 2 * D_FF), f32) * CONV_WIDTH ** -0.5,
        "conv_b": 0.02 * jax.random.normal(ks[14], (DEPTH, 2 * D_FF), f32),
        "w_down": dense(ks[15], D_FF, D_MODEL),
        "norm_ffn_post": gain(ks[16], D_MODEL),
    }


def reference(x, norm_mix_pre, w_in, q_lat_norm, w_uq, kv_lat_norm, w_ukv, out_norm_a,
              out_norm_b, w_o, norm_mix_post, norm_ffn_pre, w_up, conv_w, conv_b, w_down,
              norm_ffn_post):
    for l in range(DEPTH):
        h = _rmsnorm(x, norm_mix_pre[l])
        proj = h @ w_in[l]
        qa, ka, va, c_q, c_kv, k_r = jnp.split(proj, SPLIT_POINTS, axis=-1)
        ya = _dilated_attention(qa, ka, va)
        yb = _mla(c_q, c_kv, k_r, q_lat_norm[l], w_uq[l], kv_lat_norm[l], w_ukv[l])
        y = jnp.concatenate([_rmsnorm(ya, out_norm_a[l]), _rmsnorm(yb, out_norm_b[l])], axis=-1)
        y = y @ w_o[l]
        x = x + _rmsnorm(y, norm_mix_post[l])
        h = _rmsnorm(x, norm_ffn_pre[l])
        u = _dwconv(h @ w_up[l], conv_w[l], conv_b[l])
        g, v = u[..., :D_FF], u[..., D_FF:]
        y = (jax.nn.gelu(g, approximate=True) * v) @ w_down[l]
        x = x + _rmsnorm(y, norm_ffn_post[l])
    return x
```

```python
import functools
import math

import numpy as np
import jax
import jax.numpy as jnp
from jax import lax
from jax.experimental import pallas as pl
from jax.experimental.pallas import tpu as pltpu

F32 = jnp.float32
BF16 = jnp.bfloat16

EPS = 1e-6
NEG = -1e30
LOG2E = 1.4426950408889634

D_MODEL = 1024
N_HEADS = 8
HEAD_DIM_A = 64
WIDTH_A = N_HEADS * HEAD_DIM_A
DILATIONS = (1, 4, 16)
HALF = 64
Q_BLOCK = 128
SLAB = Q_BLOCK + 2 * HALF
SUPER = Q_BLOCK * DILATIONS[-1]
KV_PAD = HALF * DILATIONS[-1]
QK_NOPE = 64
QK_ROPE = 32
V_DIM = 64
WIDTH_B = N_HEADS * V_DIM
Q_LORA = 384
KV_LORA = 256
ROPE_BASE = 10000.0
D_FF = 2816
LANES = 128
HALO = 16

VMEM_LIMIT = 56 * 1024 * 1024


def _rms(x, g):
    return x * lax.rsqrt(jnp.mean(x * x, axis=-1, keepdims=True) + EPS) * g


def _dot(a, b):
    return jnp.dot(a, b, preferred_element_type=F32)


def _dot_nt(a, b):
    return lax.dot_general(a, b, (((1,), (1,)), ((), ())), preferred_element_type=F32)


LAT_W = Q_LORA + KV_LORA + 2 * LANES


def _inproj_kernel(x_ref, g_ref, win_ref, cos_ref, sin_ref, gq_ref, wq_ref, wqr_ref, gkv_ref, wk_ref,
                   wvt_ref, pa_ref, q_ref, k_ref, vt_ref, *, q_scale):
    h = _rms(x_ref[0], g_ref[...]).astype(BF16)
    pa_ref[0] = _dot(h, win_ref[:, :3 * WIDTH_A])
    lat = _dot(h, win_ref[:, 3 * WIDTH_A:])
    c_q = lat[:, :Q_LORA]
    c_kv = lat[:, Q_LORA:Q_LORA + KV_LORA]
    kr = lat[:, Q_LORA + KV_LORA:Q_LORA + KV_LORA + LANES]
    krr = lat[:, Q_LORA + KV_LORA + LANES:]
    cosp = cos_ref[...]
    sinp = sin_ref[...]
    lane = lax.broadcasted_iota(jnp.int32, cosp.shape, 1)
    cq_tab = (cosp + jnp.where(lane < QK_NOPE, 1.0, 0.0)) * q_scale
    sq_tab = sinp * q_scale

    cqn = _rms(c_q, gq_ref[...]).astype(BF16)
    qf = _dot(cqn, wq_ref[...])
    qr = _dot(cqn, wqr_ref[...])
    for hh in range(N_HEADS):
        sl = slice(hh * LANES, (hh + 1) * LANES)
        q_ref[0, hh] = (qf[:, sl] * cq_tab + qr[:, sl] * sq_tab).astype(BF16)

    ckvn = _rms(c_kv, gkv_ref[...]).astype(BF16)
    kf = _dot(ckvn, wk_ref[...])
    krope = kr * cosp + krr * sinp
    for hh in range(N_HEADS):
        sl = slice(hh * LANES, (hh + 1) * LANES)
        k_ref[0, hh] = (kf[:, sl] + krope).astype(BF16)
    vt_ref[0, 0] = _dot_nt(wvt_ref[...], ckvn).astype(BF16)


def _inproj(x, g, win, cosp, sinp, gq, wq, wqr, gkv, wk, wvt, *, tm, q_scale):
    b, s, d = x.shape
    nt = s // tm
    full = lambda a: pl.BlockSpec(a.shape, lambda bi, i: (0,) * a.ndim)
    return pl.pallas_call(
        functools.partial(_inproj_kernel, q_scale=q_scale),
        grid=(b, nt),
        in_specs=[
            pl.BlockSpec((1, tm, d), lambda bi, i: (bi, i, 0)),
            full(g), full(win),
            pl.BlockSpec((tm, LANES), lambda bi, i: (i, 0)),
            pl.BlockSpec((tm, LANES), lambda bi, i: (i, 0)),
            full(gq), full(wq), full(wqr), full(gkv), full(wk), full(wvt),
        ],
        out_specs=[
            pl.BlockSpec((1, tm, 3 * WIDTH_A), lambda bi, i: (bi, i, 0)),
            pl.BlockSpec((1, N_HEADS, tm, LANES), lambda bi, i: (bi, 0, i, 0)),
            pl.BlockSpec((1, N_HEADS, tm, LANES), lambda bi, i: (bi, 0, i, 0)),
            pl.BlockSpec((1, 1, WIDTH_B, tm), lambda bi, i: (bi, i, 0, 0)),
        ],
        out_shape=[
            jax.ShapeDtypeStruct((b, s, 3 * WIDTH_A), F32),
            jax.ShapeDtypeStruct((b, N_HEADS, s, LANES), BF16),
            jax.ShapeDtypeStruct((b, N_HEADS, s, LANES), BF16),
            jax.ShapeDtypeStruct((b, nt, WIDTH_B, tm), BF16),
        ],
        compiler_params=pltpu.CompilerParams(
            dimension_semantics=("parallel", "parallel"), vmem_limit_bytes=VMEM_LIMIT),
        name="inproj",
    )(x, g, win, cosp, sinp, gq, wq, wqr, gkv, wk, wvt)


def _dilated_kernel(q_ref, k_ref, v_ref, tbl_ref, o_ref, kpad, vpad, res, *, seq, q_scale):
    sb = pl.program_id(2)
    t0 = sb * SUPER

    @pl.when(sb == 0)
    def _():
        zeros = jnp.zeros((KV_PAD, LANES), F32)
        kpad[pl.ds(0, KV_PAD), :] = zeros
        vpad[pl.ds(0, KV_PAD), :] = zeros
        kpad[pl.ds(KV_PAD + seq, KV_PAD), :] = zeros
        vpad[pl.ds(KV_PAD + seq, KV_PAD), :] = zeros
        kpad[pl.ds(KV_PAD, seq), :] = k_ref[0]
        vpad[pl.ds(KV_PAD, seq), :] = v_ref[0]

    lane = lax.broadcasted_iota(jnp.int32, (SLAB, LANES), 1)
    lane_q = lax.broadcasted_iota(jnp.int32, (Q_BLOCK, LANES), 1)
    col = lax.broadcasted_iota(jnp.int32, (1, SLAB), 1)

    for gi, r in enumerate(DILATIONS):
        nblk_c = SUPER // (Q_BLOCK * r)

        def body(i, carry, gi=gi, r=r, nblk_c=nblk_c):
            c = i // nblk_c
            n = i % nblk_c
            qstart = c + n * (Q_BLOCK * r)
            kstart = t0 + qstart - HALF * r
            qb = q_ref[0, pl.ds(qstart, Q_BLOCK, stride=r), :] * q_scale
            ks = kpad[pl.ds(KV_PAD + kstart, SLAB, stride=r), :]
            vs = vpad[pl.ds(KV_PAD + kstart, SLAB, stride=r), :]
            kp = kstart + col * r
            colbias = jnp.where((kp >= 0) & (kp < seq), 0.0, NEG)
            for hh in range(2):
                hs = slice(hh * HEAD_DIM_A, (hh + 1) * HEAD_DIM_A)
                s = _dot_nt(qb[:, hs].astype(BF16), ks[:, hs].astype(BF16))
                s = s + tbl_ref[0, gi, hh] + colbias
                m = jnp.max(s, axis=-1, keepdims=True)
                p = jnp.exp2(s - m).astype(BF16)
                vh = vs if hh == 0 else pltpu.roll(vs, HEAD_DIM_A, 1)
                vaug = jnp.where(lane < HEAD_DIM_A, vh, jnp.where(lane == HEAD_DIM_A, 1.0, 0.0))
                o = _dot(p, vaug.astype(BF16))
                o = jnp.where(lane_q == HEAD_DIM_A + 1, m, o)
                res[gi, hh, pl.ds(qstart, Q_BLOCK, stride=r), :] = o
            return carry

        lax.fori_loop(0, SUPER // Q_BLOCK, body, 0)

    rows = 256

    def combine(i, carry):
        sl = pl.ds(i * rows, rows)
        outs = []
        for hh in range(2):
            rs = [res[gi, hh, sl, :] for gi in range(len(DILATIONS))]
            ms = [rr[:, HEAD_DIM_A + 1:HEAD_DIM_A + 2] for rr in rs]
            mx = functools.reduce(jnp.maximum, ms)
            acc = sum(jnp.exp2(mm - mx) * rr for mm, rr in zip(ms, rs))
            outs.append(acc[:, :HEAD_DIM_A] / acc[:, HEAD_DIM_A:HEAD_DIM_A + 1])
        o_ref[0, sl, :] = jnp.concatenate(outs, axis=-1)
        return carry

    lax.fori_loop(0, SUPER // rows, combine, 0)


def _dilated_bias_table():
    off = np.abs(np.arange(SLAB)[None, :] - HALF - np.arange(Q_BLOCK)[:, None]).astype(np.float64)
    slopes = 2.0 ** (-8.0 * np.arange(1, N_HEADS + 1) / N_HEADS)
    tbl = np.empty((N_HEADS // 2, len(DILATIONS), 2, Q_BLOCK, SLAB), np.float32)
    for hp in range(N_HEADS // 2):
        for gi, r in enumerate(DILATIONS):
            for hh in range(2):
                bias = -slopes[hp * 2 + hh] * off * r * LOG2E
                tbl[hp, gi, hh] = np.where(off <= HALF, bias, NEG)
    return tbl


def _dilated_attention(pa):
    b, s, _ = pa.shape
    nsb = s // SUPER
    npair = N_HEADS // 2
    tbl = jnp.asarray(_dilated_bias_table())
    q_scale = HEAD_DIM_A ** -0.5 * LOG2E
    return pl.pallas_call(
        functools.partial(_dilated_kernel, seq=s, q_scale=q_scale),
        grid=(b, npair, nsb),
        in_specs=[
            pl.BlockSpec((1, SUPER, LANES), lambda bi, hp, sb: (bi, sb, hp)),
            pl.BlockSpec((1, s, LANES), lambda bi, hp, sb: (bi, 0, npair + hp)),
            pl.BlockSpec((1, s, LANES), lambda bi, hp, sb: (bi, 0, 2 * npair + hp)),
            pl.BlockSpec((1, len(DILATIONS), 2, Q_BLOCK, SLAB), lambda bi, hp, sb: (hp, 0, 0, 0, 0)),
        ],
        out_specs=pl.BlockSpec((1, SUPER, LANES), lambda bi, hp, sb: (bi, sb, hp)),
        out_shape=jax.ShapeDtypeStruct((b, s, WIDTH_A), F32),
        scratch_shapes=[
            pltpu.VMEM((s + 2 * KV_PAD, LANES), F32),
            pltpu.VMEM((s + 2 * KV_PAD, LANES), F32),
            pltpu.VMEM((len(DILATIONS), 2, SUPER, LANES), F32),
        ],
        compiler_params=pltpu.CompilerParams(
            dimension_semantics=("parallel", "parallel", "arbitrary"), vmem_limit_bytes=VMEM_LIMIT),
        name="dilated_attn",
    )(pa, pa, pa, tbl)


def _mla_kernel(q_ref, k_ref, vt_ref, o_ref, *, tk):
    q = q_ref[0, 0]
    tq = q.shape[0]
    nk = k_ref.shape[2] // tk

    def body(j, carry):
        m, l, acc = carry
        ks = k_ref[0, 0, pl.ds(pl.multiple_of(j * tk, tk), tk), :]
        s = _dot_nt(ks, q)
        mn = jnp.maximum(m, jnp.max(s, axis=0, keepdims=True))
        alpha = jnp.exp2(m - mn)
        p = jnp.exp2(s - mn)
        l = alpha * l + jnp.sum(p.reshape(tk // 8, 8, tq), axis=0)
        acc = alpha * acc + _dot(vt_ref[0, j], p.astype(BF16))
        return mn, l, acc

    m0 = jnp.full((1, tq), -jnp.inf, F32)
    l0 = jnp.zeros((8, tq), F32)
    a0 = jnp.zeros((V_DIM, tq), F32)
    _, l, acc = lax.fori_loop(0, nk, body, (m0, l0, a0))
    o_ref[0] = acc / jnp.sum(l, axis=0, keepdims=True)


def _mla_attention(q, k, vt, *, tq):
    b, nh, s, _ = q.shape
    _, nkt, _, tk = vt.shape
    return pl.pallas_call(
        functools.partial(_mla_kernel, tk=tk),
        grid=(b, nh, s // tq),
        in_specs=[
            pl.BlockSpec((1, 1, tq, LANES), lambda bi, h, i: (bi, h, i, 0)),
            pl.BlockSpec((1, 1, s, LANES), lambda bi, h, i: (bi, h, 0, 0)),
            pl.BlockSpec((1, nkt, V_DIM, tk), lambda bi, h, i: (bi, 0, h, 0)),
        ],
        out_specs=pl.BlockSpec((1, V_DIM, tq), lambda bi, h, i: (bi, h, i)),
        out_shape=jax.ShapeDtypeStruct((b, nh * V_DIM, s), F32),
        compiler_params=pltpu.CompilerParams(
            dimension_semantics=("parallel", "parallel", "arbitrary"), vmem_limit_bytes=VMEM_LIMIT),
        name="mla_attn",
    )(q, k, vt)


def _outproj_kernel(ya_ref, ybt_ref, x_ref, ga_ref, gb_ref, wo_ref, gp_ref, o_ref):
    yan = _rms(ya_ref[0], ga_ref[...]).astype(BF16)
    ybn = _rms(ybt_ref[0].T, gb_ref[...]).astype(BF16)
    y = _dot(yan, wo_ref[:WIDTH_A, :]) + _dot(ybn, wo_ref[WIDTH_A:, :])
    o_ref[0] = x_ref[0] + _rms(y, gp_ref[...])


def _outproj(ya, ybt, x, ga, gb, wo, gp, *, tm):
    b, s, d = x.shape
    full = lambda a: pl.BlockSpec(a.shape, lambda bi, i: (0,) * a.ndim)
    return pl.pallas_call(
        _outproj_kernel,
        grid=(b, s // tm),
        in_specs=[
            pl.BlockSpec((1, tm, WIDTH_A), lambda bi, i: (bi, i, 0)),
            pl.BlockSpec((1, WIDTH_B, tm), lambda bi, i: (bi, 0, i)),
            pl.BlockSpec((1, tm, d), lambda bi, i: (bi, i, 0)),
            full(ga), full(gb), full(wo), full(gp),
        ],
        out_specs=pl.BlockSpec((1, tm, d), lambda bi, i: (bi, i, 0)),
        out_shape=jax.ShapeDtypeStruct((b, s, d), F32),
        compiler_params=pltpu.CompilerParams(
            dimension_semantics=("parallel", "parallel"), vmem_limit_bytes=VMEM_LIMIT),
        name="outproj",
    )(ya, ybt, x, ga, gb, wo, gp)


def _ffn_kernel(xp_ref, x_ref, xn_ref, g_ref, wg_ref, wv_ref, cwg_ref, cwv_ref, cbg_ref, cbv_ref, wd_ref,
                gp_ref, o_ref, hs, ug, uv, acc, *, tm):
    i = pl.program_id(1)
    j = pl.program_id(2)

    @pl.when(j == 0)
    def _():
        g = g_ref[...]
        hp = _rms(xp_ref[0], g) * jnp.where(i > 0, 1.0, 0.0)
        hn = _rms(xn_ref[0], g) * jnp.where(i < pl.num_programs(1) - 1, 1.0, 0.0)
        hs[pl.ds(0, HALO), :] = hp.astype(BF16)
        hs[pl.ds(HALO, tm), :] = _rms(x_ref[0], g).astype(BF16)
        hs[pl.ds(HALO + tm, HALO), :] = hn.astype(BF16)
        acc[...] = jnp.zeros_like(acc)

    h = hs[...]
    ug[...] = _dot(h, wg_ref[...])
    uv[...] = _dot(h, wv_ref[...])

    def conv(u, cw_ref, cb_ref):
        return (u[pl.ds(HALO - 1, tm), :] * cw_ref[0:1, :] + u[pl.ds(HALO, tm), :] * cw_ref[1:2, :]
                + u[pl.ds(HALO + 1, tm), :] * cw_ref[2:3, :] + cb_ref[...])

    a = jax.nn.gelu(conv(ug, cwg_ref, cbg_ref), approximate=True) * conv(uv, cwv_ref, cbv_ref)
    acc[...] += _dot(a.astype(BF16), wd_ref[...])

    @pl.when(j == pl.num_programs(2) - 1)
    def _():
        o_ref[0] = x_ref[0] + _rms(acc[...], gp_ref[...])


def _ffn(x, g, wup, cw, cb, wd, gp, *, tm, tc):
    b, s, d = x.shape
    nt = s // tm
    nc = D_FF // tc
    hb = tm // HALO
    last = s // HALO - 1
    return pl.pallas_call(
        functools.partial(_ffn_kernel, tm=tm),
        grid=(b, nt, nc),
        in_specs=[
            pl.BlockSpec((1, HALO, d), lambda bi, i, j: (bi, jnp.maximum(i * hb - 1, 0), 0)),
            pl.BlockSpec((1, tm, d), lambda bi, i, j: (bi, i, 0)),
            pl.BlockSpec((1, HALO, d), lambda bi, i, j: (bi, jnp.minimum((i + 1) * hb, last), 0)),
            pl.BlockSpec((1, d), lambda bi, i, j: (0, 0)),
            pl.BlockSpec((d, tc), lambda bi, i, j: (0, j)),
            pl.BlockSpec((d, tc), lambda bi, i, j: (0, nc + j)),
            pl.BlockSpec((3, tc), lambda bi, i, j: (0, j)),
            pl.BlockSpec((3, tc), lambda bi, i, j: (0, nc + j)),
            pl.BlockSpec((1, tc), lambda bi, i, j: (0, j)),
            pl.BlockSpec((1, tc), lambda bi, i, j: (0, nc + j)),
            pl.BlockSpec((tc, d), lambda bi, i, j: (j, 0)),
            pl.BlockSpec((1, d), lambda bi, i, j: (0, 0)),
        ],
        out_specs=pl.BlockSpec((1, tm, d), lambda bi, i, j: (bi, i, 0)),
        out_shape=jax.ShapeDtypeStruct((b, s, d), F32),
        scratch_shapes=[
            pltpu.VMEM((tm + 2 * HALO, d), BF16),
            pltpu.VMEM((tm + 2 * HALO, tc), F32),
            pltpu.VMEM((tm + 2 * HALO, tc), F32),
            pltpu.VMEM((tm, d), F32),
        ],
        compiler_params=pltpu.CompilerParams(
            dimension_semantics=("parallel", "parallel", "arbitrary"), vmem_limit_bytes=VMEM_LIMIT),
        name="convffn",
    )(x, x, x, g, wup, wup, cw, cw, cb, cb, wd, gp)


def _rot_cols(w):
    half = w.shape[-1] // 2
    return jnp.concatenate([-w[..., half:], w[..., :half]], axis=-1)


def _pad_cols(w, before, total):
    return jnp.pad(w, [(0, 0)] * (w.ndim - 1) + [(before, total - before - w.shape[-1])])


def _layer(x, norm_mix_pre, w_in, q_lat_norm, w_uq, kv_lat_norm, w_ukv, out_norm_a, out_norm_b, w_o,
           norm_mix_post, norm_ffn_pre, w_up, conv_w, conv_b, w_down, norm_ffn_post):
    b, s, d = x.shape
    row = lambda v: v.reshape(1, -1).astype(F32)

    c0 = 3 * WIDTH_A + Q_LORA + KV_LORA
    w_kr = w_in[:, c0:]
    win = jnp.concatenate(
        [w_in[:, :c0], _pad_cols(w_kr, QK_NOPE, LANES), _pad_cols(_rot_cols(w_kr), QK_NOPE, LANES)],
        axis=1).astype(BF16)
    wq3 = w_uq.reshape(Q_LORA, N_HEADS, QK_NOPE + QK_ROPE)
    wq = _pad_cols(wq3, 0, LANES).reshape(Q_LORA, N_HEADS * LANES).astype(BF16)
    wqr = _pad_cols(_rot_cols(wq3[..., QK_NOPE:]), QK_NOPE, LANES).reshape(Q_LORA, N_HEADS * LANES).astype(BF16)
    wkv3 = w_ukv.reshape(KV_LORA, N_HEADS, QK_NOPE + V_DIM)
    wk = _pad_cols(wkv3[..., :QK_NOPE], 0, LANES).reshape(KV_LORA, N_HEADS * LANES).astype(BF16)
    wvt = wkv3[..., QK_NOPE:].reshape(KV_LORA, WIDTH_B).T.astype(BF16)

    pos = jnp.arange(s, dtype=F32)
    inv_freq = jnp.exp(-math.log(ROPE_BASE) * jnp.arange(0, QK_ROPE, 2, dtype=F32) / QK_ROPE)
    ang = pos[:, None] * inv_freq[None, :]
    cosp = _pad_cols(jnp.tile(jnp.cos(ang), (1, 2)), QK_NOPE, LANES)
    sinp = _pad_cols(jnp.tile(jnp.sin(ang), (1, 2)), QK_NOPE, LANES)

    q_scale = (QK_NOPE + QK_ROPE) ** -0.5 * LOG2E
    pa, q, k, vt = _inproj(x, row(norm_mix_pre), win, cosp, sinp, row(q_lat_norm), wq, wqr,
                           row(kv_lat_norm), wk, wvt, tm=512, q_scale=q_scale)
    ya = _dilated_attention(pa)
    ybt = _mla_attention(q, k, vt, tq=1024)
    x1 = _outproj(ya, ybt, x, row(out_norm_a), row(out_norm_b), w_o.astype(BF16), row(norm_mix_post), tm=512)
    return _ffn(x1, row(norm_ffn_pre), w_up.astype(BF16), conv_w, conv_b.reshape(1, -1),
                w_down.astype(BF16), row(norm_ffn_post), tm=1024, tc=256)


def kernel(x, norm_mix_pre, w_in, q_lat_norm, w_uq, kv_lat_norm, w_ukv, out_norm_a, out_norm_b, w_o,
           norm_mix_post, norm_ffn_pre, w_up, conv_w, conv_b, w_down, norm_ffn_post):
    for l in range(norm_mix_pre.shape[0]):
        x = _layer(x, norm_mix_pre[l], w_in[l], q_lat_norm[l], w_uq[l], kv_lat_norm[l], w_ukv[l],
                   out_norm_a[l], out_norm_b[l], w_o[l], norm_mix_post[l], norm_ffn_pre[l], w_up[l],
                   conv_w[l], conv_b[l], w_down[l], norm_ffn_post[l])
    return x
```

```python
import functools
import math

import numpy as np
import jax
import jax.numpy as jnp
from jax import lax
from jax.experimental import pallas as pl
from jax.experimental.pallas import tpu as pltpu

F32 = jnp.float32
BF16 = jnp.bfloat16

EPS = 1e-6
NEG = -1e30
LOG2E = 1.4426950408889634

D_MODEL = 1024
N_HEADS = 8
HEAD_DIM_A = 64
WIDTH_A = N_HEADS * HEAD_DIM_A
DILATIONS = (1, 4, 16)
HALF = 64
Q_BLOCK = 128
SLAB = Q_BLOCK + 2 * HALF
SUPER = Q_BLOCK * DILATIONS[-1]
KV_PAD = HALF * DILATIONS[-1]
QK_NOPE = 64
QK_ROPE = 32
V_DIM = 64
WIDTH_B = N_HEADS * V_DIM
Q_LORA = 384
KV_LORA = 256
ROPE_BASE = 10000.0
D_FF = 2816
LANES = 128
HALO = 16

VMEM_LIMIT = 56 * 1024 * 1024


def _rms(x, g):
    return x * lax.rsqrt(jnp.mean(x * x, axis=-1, keepdims=True) + EPS) * g


def _dot(a, b):
    return jnp.dot(a, b, preferred_element_type=F32)


def _dot_nt(a, b):
    return lax.dot_general(a, b, (((1,), (1,)), ((), ())), preferred_element_type=F32)


LAT_W = Q_LORA + KV_LORA + 2 * LANES


def _inproj_kernel(x_ref, g_ref, win_ref, cos_ref, sin_ref, gq_ref, wq_ref, wqr_ref, gkv_ref, wk_ref,
                   wvt_ref, pa_ref, q_ref, k_ref, vt_ref, *, q_scale):
    h = _rms(x_ref[0], g_ref[...]).astype(BF16)
    pa_ref[0] = _dot(h, win_ref[:, :3 * WIDTH_A])
    lat = _dot(h, win_ref[:, 3 * WIDTH_A:])
    c_q = lat[:, :Q_LORA]
    c_kv = lat[:, Q_LORA:Q_LORA + KV_LORA]
    kr = lat[:, Q_LORA + KV_LORA:Q_LORA + KV_LORA + LANES]
    krr = lat[:, Q_LORA + KV_LORA + LANES:]
    cosp = cos_ref[...]
    sinp = sin_ref[...]
    lane = lax.broadcasted_iota(jnp.int32, cosp.shape, 1)
    cq_tab = (cosp + jnp.where(lane < QK_NOPE, 1.0, 0.0)) * q_scale
    sq_tab = sinp * q_scale

    cqn = _rms(c_q, gq_ref[...]).astype(BF16)
    qf = _dot(cqn, wq_ref[...])
    qr = _dot(cqn, wqr_ref[...])
    for hh in range(N_HEADS):
        sl = slice(hh * LANES, (hh + 1) * LANES)
        q_ref[0, hh] = (qf[:, sl] * cq_tab + qr[:, sl] * sq_tab).astype(BF16)

    ckvn = _rms(c_kv, gkv_ref[...]).astype(BF16)
    kf = _dot(ckvn, wk_ref[...])
    krope = kr * cosp + krr * sinp
    for hh in range(N_HEADS):
        sl = slice(hh * LANES, (hh + 1) * LANES)
        k_ref[0, hh] = (kf[:, sl] + krope).astype(BF16)
    vt_ref[0, 0] = _dot_nt(wvt_ref[...], ckvn).astype(BF16)


def _inproj(x, g, win, cosp, sinp, gq, wq, wqr, gkv, wk, wvt, *, tm, q_scale):
    b, s, d = x.shape
    nt = s // tm
    full = lambda a: pl.BlockSpec(a.shape, lambda bi, i: (0,) * a.ndim)
    return pl.pallas_call(
        functools.partial(_inproj_kernel, q_scale=q_scale),
        grid=(b, nt),
        in_specs=[
            pl.BlockSpec((1, tm, d), lambda bi, i: (bi, i, 0)),
            full(g), full(win),
            pl.BlockSpec((tm, LANES), lambda bi, i: (i, 0)),
            pl.BlockSpec((tm, LANES), lambda bi, i: (i, 0)),
            full(gq), full(wq), full(wqr), full(gkv), full(wk), full(wvt),
        ],
        out_specs=[
            pl.BlockSpec((1, tm, 3 * WIDTH_A), lambda bi, i: (bi, i, 0)),
            pl.BlockSpec((1, N_HEADS, tm, LANES), lambda bi, i: (bi, 0, i, 0)),
            pl.BlockSpec((1, N_HEADS, tm, LANES), lambda bi, i: (bi, 0, i, 0)),
            pl.BlockSpec((1, 1, WIDTH_B, tm), lambda bi, i: (bi, i, 0, 0)),
        ],
        out_shape=[
            jax.ShapeDtypeStruct((b, s, 3 * WIDTH_A), F32),
            jax.ShapeDtypeStruct((b, N_HEADS, s, LANES), BF16),
            jax.ShapeDtypeStruct((b, N_HEADS, s, LANES), BF16),
            jax.ShapeDtypeStruct((b, nt, WIDTH_B, tm), BF16),
        ],
        compiler_params=pltpu.CompilerParams(
            dimension_semantics=("parallel", "parallel"), vmem_limit_bytes=VMEM_LIMIT),
        name="inproj",
    )(x, g, win, cosp, sinp, gq, wq, wqr, gkv, wk, wvt)


def _dilated_kernel(q_ref, k_ref, v_ref, tbl_ref, o_ref, kpad, vpad, res, *, seq, q_scale):
    sb = pl.program_id(2)
    t0 = sb * SUPER

    @pl.when(sb == 0)
    def _():
        zeros = jnp.zeros((KV_PAD, LANES), F32)
        kpad[pl.ds(0, KV_PAD), :] = zeros
        vpad[pl.ds(0, KV_PAD), :] = zeros
        kpad[pl.ds(KV_PAD + seq, KV_PAD), :] = zeros
        vpad[pl.ds(KV_PAD + seq, KV_PAD), :] = zeros
        kpad[pl.ds(KV_PAD, seq), :] = k_ref[0]
        vpad[pl.ds(KV_PAD, seq), :] = v_ref[0]

    lane = lax.broadcasted_iota(jnp.int32, (SLAB, LANES), 1)
    lane_q = lax.broadcasted_iota(jnp.int32, (Q_BLOCK, LANES), 1)
    col = lax.broadcasted_iota(jnp.int32, (1, SLAB), 1)

    for gi, r in enumerate(DILATIONS):
        nblk_c = SUPER // (Q_BLOCK * r)

        def body(i, carry, gi=gi, r=r, nblk_c=nblk_c):
            c = i // nblk_c
            n = i % nblk_c
            qstart = c + n * (Q_BLOCK * r)
            kstart = t0 + qstart - HALF * r
            qb = q_ref[0, pl.ds(qstart, Q_BLOCK, stride=r), :] * q_scale
            ks = kpad[pl.ds(KV_PAD + kstart, SLAB, stride=r), :]
            vs = vpad[pl.ds(KV_PAD + kstart, SLAB, stride=r), :]
            kp = kstart + col * r
            colbias = jnp.where((kp >= 0) & (kp < seq), 0.0, NEG)
            for hh in range(2):
                hs = slice(hh * HEAD_DIM_A, (hh + 1) * HEAD_DIM_A)
                s = _dot_nt(qb[:, hs].astype(BF16), ks[:, hs].astype(BF16))
                s = s + tbl_ref[0, gi, hh] + colbias
                m = jnp.max(s, axis=-1, keepdims=True)
                p = jnp.exp2(s - m).astype(BF16)
                vh = vs if hh == 0 else pltpu.roll(vs, HEAD_DIM_A, 1)
                vaug = jnp.where(lane < HEAD_DIM_A, vh, jnp.where(lane == HEAD_DIM_A, 1.0, 0.0))
                o = _dot(p, vaug.astype(BF16))
                o = jnp.where(lane_q == HEAD_DIM_A + 1, m, o)
                res[gi, hh, pl.ds(qstart, Q_BLOCK, stride=r), :] = o
            return carry

        lax.fori_loop(0, SUPER // Q_BLOCK, body, 0, unroll=8)

    rows = 256

    def combine(i, carry):
        sl = pl.ds(i * rows, rows)
        outs = []
        for hh in range(2):
            rs = [res[gi, hh, sl, :] for gi in range(len(DILATIONS))]
            ms = [rr[:, HEAD_DIM_A + 1:HEAD_DIM_A + 2] for rr in rs]
            mx = functools.reduce(jnp.maximum, ms)
            acc = sum(jnp.exp2(mm - mx) * rr for mm, rr in zip(ms, rs))
            outs.append(acc[:, :HEAD_DIM_A] / acc[:, HEAD_DIM_A:HEAD_DIM_A + 1])
        o_ref[0, sl, :] = jnp.concatenate(outs, axis=-1)
        return carry

    lax.fori_loop(0, SUPER // rows, combine, 0)


def _dilated_bias_table():
    off = np.abs(np.arange(SLAB)[None, :] - HALF - np.arange(Q_BLOCK)[:, None]).astype(np.float64)
    slopes = 2.0 ** (-8.0 * np.arange(1, N_HEADS + 1) / N_HEADS)
    tbl = np.empty((N_HEADS // 2, len(DILATIONS), 2, Q_BLOCK, SLAB), np.float32)
    for hp in range(N_HEADS // 2):
        for gi, r in enumerate(DILATIONS):
            for hh in range(2):
                bias = -slopes[hp * 2 + hh] * off * r * LOG2E
                tbl[hp, gi, hh] = np.where(off <= HALF, bias, NEG)
    return tbl


def _dilated_attention(pa):
    b, s, _ = pa.shape
    nsb = s // SUPER
    npair = N_HEADS // 2
    tbl = jnp.asarray(_dilated_bias_table())
    q_scale = HEAD_DIM_A ** -0.5 * LOG2E
    return pl.pallas_call(
        functools.partial(_dilated_kernel, seq=s, q_scale=q_scale),
        grid=(b, npair, nsb),
        in_specs=[
            pl.BlockSpec((1, SUPER, LANES), lambda bi, hp, sb: (bi, sb, hp)),
            pl.BlockSpec((1, s, LANES), lambda bi, hp, sb: (bi, 0, npair + hp)),
            pl.BlockSpec((1, s, LANES), lambda bi, hp, sb: (bi, 0, 2 * npair + hp)),
            pl.BlockSpec((1, len(DILATIONS), 2, Q_BLOCK, SLAB), lambda bi, hp, sb: (hp, 0, 0, 0, 0)),
        ],
        out_specs=pl.BlockSpec((1, SUPER, LANES), lambda bi, hp, sb: (bi, sb, hp)),
        out_shape=jax.ShapeDtypeStruct((b, s, WIDTH_A), F32),
        scratch_shapes=[
            pltpu.VMEM((s + 2 * KV_PAD, LANES), F32),
            pltpu.VMEM((s + 2 * KV_PAD, LANES), F32),
            pltpu.VMEM((len(DILATIONS), 2, SUPER, LANES), F32),
        ],
        compiler_params=pltpu.CompilerParams(
            dimension_semantics=("parallel", "parallel", "arbitrary"), vmem_limit_bytes=VMEM_LIMIT),
        name="dilated_attn",
    )(pa, pa, pa, tbl)


def _mla_kernel(q_ref, k_ref, vt_ref, o_ref, s_a, s_b, p_a, p_b, *, tk):
    q = q_ref[0, 0]
    tq = q.shape[0]
    nk = k_ref.shape[2] // tk

    def scores(j, s_buf):
        ks = k_ref[0, 0, pl.ds(pl.multiple_of(j * tk, tk), tk), :]
        s = _dot_nt(ks, q)
        s_buf[...] = s
        return jnp.max(s, axis=0, keepdims=True)

    def softmax(s_buf, p_buf, tmax, m, l):
        mn = jnp.maximum(m, tmax)
        alpha = jnp.exp2(m - mn)
        p = jnp.exp2(s_buf[...] - mn)
        l = alpha * l + jnp.sum(p.reshape(tk // 8, 8, tq), axis=0)
        p_buf[...] = p.astype(BF16)
        return mn, l, alpha

    def pv(j, p_buf, alpha, acc):
        return alpha * acc + _dot(vt_ref[0, j], p_buf[...])

    def trip(i, carry, last=False):
        t_a, al_b, m, l, acc = carry
        t_b = scores(2 * i + 1, s_b)
        m, l, al_a = softmax(s_a, p_a, t_a, m, l)
        acc = pv(jnp.maximum(2 * i - 1, 0), p_b, al_b, acc)
        if not last:
            t_a = scores(2 * i + 2, s_a)
        m, l, al_b = softmax(s_b, p_b, t_b, m, l)
        acc = pv(2 * i, p_a, al_a, acc)
        return t_a, al_b, m, l, acc

    p_b[...] = jnp.zeros_like(p_b)
    carry = (scores(0, s_a), jnp.ones((1, tq), F32), jnp.full((1, tq), -jnp.inf, F32),
             jnp.zeros((8, tq), F32), jnp.zeros((V_DIM, tq), F32))
    carry = lax.fori_loop(0, nk // 2 - 1, trip, carry)
    _, al_b, _, l, acc = trip(nk // 2 - 1, carry, last=True)
    acc = pv(nk - 1, p_b, al_b, acc)
    o_ref[0] = acc / jnp.sum(l, axis=0, keepdims=True)


def _mla_attention(q, k, vt, *, tq):
    b, nh, s, _ = q.shape
    _, nkt, _, tk = vt.shape
    return pl.pallas_call(
        functools.partial(_mla_kernel, tk=tk),
        grid=(b, nh, s // tq),
        in_specs=[
            pl.BlockSpec((1, 1, tq, LANES), lambda bi, h, i: (bi, h, i, 0)),
            pl.BlockSpec((1, 1, s, LANES), lambda bi, h, i: (bi, h, 0, 0)),
            pl.BlockSpec((1, nkt, V_DIM, tk), lambda bi, h, i: (bi, 0, h, 0)),
        ],
        out_specs=pl.BlockSpec((1, V_DIM, tq), lambda bi, h, i: (bi, h, i)),
        out_shape=jax.ShapeDtypeStruct((b, nh * V_DIM, s), F32),
        scratch_shapes=[pltpu.VMEM((tk, tq), F32), pltpu.VMEM((tk, tq), F32),
                        pltpu.VMEM((tk, tq), BF16), pltpu.VMEM((tk, tq), BF16)],
        compiler_params=pltpu.CompilerParams(
            dimension_semantics=("parallel", "parallel", "arbitrary"), vmem_limit_bytes=VMEM_LIMIT),
        name="mla_attn",
    )(q, k, vt)


def _outproj_kernel(ya_ref, ybt_ref, x_ref, ga_ref, gb_ref, wo_ref, gp_ref, o_ref):
    yan = _rms(ya_ref[0], ga_ref[...]).astype(BF16)
    ybn = _rms(ybt_ref[0].T, gb_ref[...]).astype(BF16)
    y = _dot(yan, wo_ref[:WIDTH_A, :]) + _dot(ybn, wo_ref[WIDTH_A:, :])
    o_ref[0] = x_ref[0] + _rms(y, gp_ref[...])


def _outproj(ya, ybt, x, ga, gb, wo, gp, *, tm):
    b, s, d = x.shape
    full = lambda a: pl.BlockSpec(a.shape, lambda bi, i: (0,) * a.ndim)
    return pl.pallas_call(
        _outproj_kernel,
        grid=(b, s // tm),
        in_specs=[
            pl.BlockSpec((1, tm, WIDTH_A), lambda bi, i: (bi, i, 0)),
            pl.BlockSpec((1, WIDTH_B, tm), lambda bi, i: (bi, 0, i)),
            pl.BlockSpec((1, tm, d), lambda bi, i: (bi, i, 0)),
            full(ga), full(gb), full(wo), full(gp),
        ],
        out_specs=pl.BlockSpec((1, tm, d), lambda bi, i: (bi, i, 0)),
        out_shape=jax.ShapeDtypeStruct((b, s, d), F32),
        compiler_params=pltpu.CompilerParams(
            dimension_semantics=("parallel", "parallel"), vmem_limit_bytes=VMEM_LIMIT),
        name="outproj",
    )(ya, ybt, x, ga, gb, wo, gp)


def _ffn_kernel(xp_ref, x_ref, xn_ref, g_ref, wg_ref, wv_ref, cwg_ref, cwv_ref, cbg_ref, cbv_ref, wd_ref,
                gp_ref, o_ref, hs, ug, uv, acc, *, tm):
    i = pl.program_id(1)
    j = pl.program_id(2)

    @pl.when(j == 0)
    def _():
        g = g_ref[...]
        hp = _rms(xp_ref[0], g) * jnp.where(i > 0, 1.0, 0.0)
        hn = _rms(xn_ref[0], g) * jnp.where(i < pl.num_programs(1) - 1, 1.0, 0.0)
        hs[pl.ds(0, HALO), :] = hp.astype(BF16)
        hs[pl.ds(HALO, tm), :] = _rms(x_ref[0], g).astype(BF16)
        hs[pl.ds(HALO + tm, HALO), :] = hn.astype(BF16)
        acc[...] = jnp.zeros_like(acc)

    h = hs[...]
    ug[...] = _dot(h, wg_ref[...])
    uv[...] = _dot(h, wv_ref[...])

    def conv(u, cw_ref, cb_ref):
        return (u[pl.ds(HALO - 1, tm), :] * cw_ref[0:1, :] + u[pl.ds(HALO, tm), :] * cw_ref[1:2, :]
                + u[pl.ds(HALO + 1, tm), :] * cw_ref[2:3, :] + cb_ref[...])

    a = jax.nn.gelu(conv(ug, cwg_ref, cbg_ref), approximate=True) * conv(uv, cwv_ref, cbv_ref)
    acc[...] += _dot(a.astype(BF16), wd_ref[...])

    @pl.when(j == pl.num_programs(2) - 1)
    def _():
        o_ref[0] = x_ref[0] + _rms(acc[...], gp_ref[...])


def _ffn(x, g, wup, cw, cb, wd, gp, *, tm, tc):
    b, s, d = x.shape
    nt = s // tm
    nc = D_FF // tc
    hb = tm // HALO
    last = s // HALO - 1
    return pl.pallas_call(
        functools.partial(_ffn_kernel, tm=tm),
        grid=(b, nt, nc),
        in_specs=[
            pl.BlockSpec((1, HALO, d), lambda bi, i, j: (bi, jnp.maximum(i * hb - 1, 0), 0)),
            pl.BlockSpec((1, tm, d), lambda bi, i, j: (bi, i, 0)),
            pl.BlockSpec((1, HALO, d), lambda bi, i, j: (bi, jnp.minimum((i + 1) * hb, last), 0)),
            pl.BlockSpec((1, d), lambda bi, i, j: (0, 0)),
            pl.BlockSpec((d, tc), lambda bi, i, j: (0, j)),
            pl.BlockSpec((d, tc), lambda bi, i, j: (0, nc + j)),
            pl.BlockSpec((3, tc), lambda bi, i, j: (0, j)),
            pl.BlockSpec((3, tc), lambda bi, i, j: (0, nc + j)),
            pl.BlockSpec((1, tc), lambda bi, i, j: (0, j)),
            pl.BlockSpec((1, tc), lambda bi, i, j: (0, nc + j)),
            pl.BlockSpec((tc, d), lambda bi, i, j: (j, 0)),
            pl.BlockSpec((1, d), lambda bi, i, j: (0, 0)),
        ],
        out_specs=pl.BlockSpec((1, tm, d), lambda bi, i, j: (bi, i, 0)),
        out_shape=jax.ShapeDtypeStruct((b, s, d), F32),
        scratch_shapes=[
            pltpu.VMEM((tm + 2 * HALO, d), BF16),
            pltpu.VMEM((tm + 2 * HALO, tc), F32),
            pltpu.VMEM((tm + 2 * HALO, tc), F32),
            pltpu.VMEM((tm, d), F32),
        ],
        compiler_params=pltpu.CompilerParams(
            dimension_semantics=("parallel", "parallel", "arbitrary"), vmem_limit_bytes=VMEM_LIMIT),
        name="convffn",
    )(x, x, x, g, wup, wup, cw, cw, cb, cb, wd, gp)


def _rot_cols(w):
    half = w.shape[-1] // 2
    return jnp.concatenate([-w[..., half:], w[..., :half]], axis=-1)


def _pad_cols(w, before, total):
    return jnp.pad(w, [(0, 0)] * (w.ndim - 1) + [(before, total - before - w.shape[-1])])


def _layer(x, norm_mix_pre, w_in, q_lat_norm, w_uq, kv_lat_norm, w_ukv, out_norm_a, out_norm_b, w_o,
           norm_mix_post, norm_ffn_pre, w_up, conv_w, conv_b, w_down, norm_ffn_post):
    b, s, d = x.shape
    row = lambda v: v.reshape(1, -1).astype(F32)

    c0 = 3 * WIDTH_A + Q_LORA + KV_LORA
    w_kr = w_in[:, c0:]
    win = jnp.concatenate(
        [w_in[:, :c0], _pad_cols(w_kr, QK_NOPE, LANES), _pad_cols(_rot_cols(w_kr), QK_NOPE, LANES)],
        axis=1).astype(BF16)
    wq3 = w_uq.reshape(Q_LORA, N_HEADS, QK_NOPE + QK_ROPE)
    wq = _pad_cols(wq3, 0, LANES).reshape(Q_LORA, N_HEADS * LANES).astype(BF16)
    wqr = _pad_cols(_rot_cols(wq3[..., QK_NOPE:]), QK_NOPE, LANES).reshape(Q_LORA, N_HEADS * LANES).astype(BF16)
    wkv3 = w_ukv.reshape(KV_LORA, N_HEADS, QK_NOPE + V_DIM)
    wk = _pad_cols(wkv3[..., :QK_NOPE], 0, LANES).reshape(KV_LORA, N_HEADS * LANES).astype(BF16)
    wvt = wkv3[..., QK_NOPE:].reshape(KV_LORA, WIDTH_B).T.astype(BF16)

    pos = jnp.arange(s, dtype=F32)
    inv_freq = jnp.exp(-math.log(ROPE_BASE) * jnp.arange(0, QK_ROPE, 2, dtype=F32) / QK_ROPE)
    ang = pos[:, None] * inv_freq[None, :]
    cosp = _pad_cols(jnp.tile(jnp.cos(ang), (1, 2)), QK_NOPE, LANES)
    sinp = _pad_cols(jnp.tile(jnp.sin(ang), (1, 2)), QK_NOPE, LANES)

    q_scale = (QK_NOPE + QK_ROPE) ** -0.5 * LOG2E
    pa, q, k, vt = _inproj(x, row(norm_mix_pre), win, cosp, sinp, row(q_lat_norm), wq, wqr,
                           row(kv_lat_norm), wk, wvt, tm=512, q_scale=q_scale)
    ya = _dilated_attention(pa)
    ybt = _mla_attention(q, k, vt, tq=2048)
    x1 = _outproj(ya, ybt, x, row(out_norm_a), row(out_norm_b), w_o.astype(BF16), row(norm_mix_post), tm=512)
    return _ffn(x1, row(norm_ffn_pre), w_up.astype(BF16), conv_w, conv_b.reshape(1, -1),
                w_down.astype(BF16), row(norm_ffn_post), tm=1024, tc=256)


def kernel(x, norm_mix_pre, w_in, q_lat_norm, w_uq, kv_lat_norm, w_ukv, out_norm_a, out_norm_b, w_o,
           norm_mix_post, norm_ffn_pre, w_up, conv_w, conv_b, w_down, norm_ffn_post):
    for l in range(norm_mix_pre.shape[0]):
        x = _layer(x, norm_mix_pre[l], w_in[l], q_lat_norm[l], w_uq[l], kv_lat_norm[l], w_ukv[l],
                   out_norm_a[l], out_norm_b[l], w_o[l], norm_mix_post[l], norm_ffn_pre[l], w_up[l],
                   conv_w[l], conv_b[l], w_down[l], norm_ffn_post[l])
    return x
```

```python
import functools
import math

import numpy as np
import jax
import jax.numpy as jnp
from jax import lax
from jax.experimental import pallas as pl
from jax.experimental.pallas import tpu as pltpu

F32 = jnp.float32
BF16 = jnp.bfloat16

EPS = 1e-6
NEG = -1e30
LOG2E = 1.4426950408889634

D_MODEL = 1024
N_HEADS = 8
HEAD_DIM_A = 64
WIDTH_A = N_HEADS * HEAD_DIM_A
DILATIONS = (1, 4, 16)
HALF = 64
Q_BLOCK = 128
SLAB = Q_BLOCK + 2 * HALF
SUPER = Q_BLOCK * DILATIONS[-1]
KV_PAD = HALF * DILATIONS[-1]
QK_NOPE = 64
QK_ROPE = 32
V_DIM = 64
WIDTH_B = N_HEADS * V_DIM
Q_LORA = 384
KV_LORA = 256
ROPE_BASE = 10000.0
D_FF = 2816
FF_CHUNK = 256
LANES = 128
HALO = 16
ONES_ROWS = 16

VMEM_LIMIT = 56 * 1024 * 1024


def _rms(x, g):
    return x * lax.rsqrt(jnp.mean(x * x, axis=-1, keepdims=True) + EPS) * g


def _dot(a, b):
    return jnp.dot(a, b, preferred_element_type=F32)


def _dot_nt(a, b):
    return lax.dot_general(a, b, (((1,), (1,)), ((), ())), preferred_element_type=F32)


LAT_W = Q_LORA + KV_LORA + 2 * LANES


def _inproj_kernel(x_ref, g_ref, win_ref, cos_ref, sin_ref, gq_ref, wq_ref, wqr_ref, gkv_ref, wk_ref,
                   wvt_ref, pa_ref, q_ref, k_ref, vt_ref, *, q_scale):
    h = _rms(x_ref[0], g_ref[...]).astype(BF16)
    pa_ref[0] = _dot(h, win_ref[:, :3 * WIDTH_A])
    lat = _dot(h, win_ref[:, 3 * WIDTH_A:])
    c_q = lat[:, :Q_LORA]
    c_kv = lat[:, Q_LORA:Q_LORA + KV_LORA]
    kr = lat[:, Q_LORA + KV_LORA:Q_LORA + KV_LORA + LANES]
    krr = lat[:, Q_LORA + KV_LORA + LANES:]
    cosp = cos_ref[...]
    sinp = sin_ref[...]
    lane = lax.broadcasted_iota(jnp.int32, cosp.shape, 1)
    cq_tab = (cosp + jnp.where(lane < QK_NOPE, 1.0, 0.0)) * q_scale
    sq_tab = sinp * q_scale

    cqn = _rms(c_q, gq_ref[...]).astype(BF16)
    qf = _dot(cqn, wq_ref[...])
    qr = _dot(cqn, wqr_ref[...])
    for hh in range(N_HEADS):
        sl = slice(hh * LANES, (hh + 1) * LANES)
        q_ref[0, hh] = (qf[:, sl] * cq_tab + qr[:, sl] * sq_tab).astype(BF16)

    ckvn = _rms(c_kv, gkv_ref[...]).astype(BF16)
    kf = _dot(ckvn, wk_ref[...])
    krope = kr * cosp + krr * sinp
    for hh in range(N_HEADS):
        sl = slice(hh * LANES, (hh + 1) * LANES)
        k_ref[0, hh] = (kf[:, sl] + krope).astype(BF16)
    vt_ref[0, 0] = _dot_nt(wvt_ref[...], ckvn).astype(BF16)


def _inproj(x, g, win, cosp, sinp, gq, wq, wqr, gkv, wk, wvt, *, tm, q_scale):
    b, s, d = x.shape
    nt = s // tm
    full = lambda a: pl.BlockSpec(a.shape, lambda bi, i: (0,) * a.ndim)
    return pl.pallas_call(
        functools.partial(_inproj_kernel, q_scale=q_scale),
        grid=(b, nt),
        in_specs=[
            pl.BlockSpec((1, tm, d), lambda bi, i: (bi, i, 0)),
            full(g), full(win),
            pl.BlockSpec((tm, LANES), lambda bi, i: (i, 0)),
            pl.BlockSpec((tm, LANES), lambda bi, i: (i, 0)),
            full(gq), full(wq), full(wqr), full(gkv), full(wk), full(wvt),
        ],
        out_specs=[
            pl.BlockSpec((1, tm, 3 * WIDTH_A), lambda bi, i: (bi, i, 0)),
            pl.BlockSpec((1, N_HEADS, tm, LANES), lambda bi, i: (bi, 0, i, 0)),
            pl.BlockSpec((1, N_HEADS, tm, LANES), lambda bi, i: (bi, 0, i, 0)),
            pl.BlockSpec((1, 1, WIDTH_B, tm), lambda bi, i: (bi, i, 0, 0)),
        ],
        out_shape=[
            jax.ShapeDtypeStruct((b, s, 3 * WIDTH_A), F32),
            jax.ShapeDtypeStruct((b, N_HEADS, s, LANES), BF16),
            jax.ShapeDtypeStruct((b, N_HEADS, s, LANES), BF16),
            jax.ShapeDtypeStruct((b, nt, WIDTH_B, tm), BF16),
        ],
        compiler_params=pltpu.CompilerParams(
            dimension_semantics=("parallel", "parallel"), vmem_limit_bytes=VMEM_LIMIT),
        name="inproj",
    )(x, g, win, cosp, sinp, gq, wq, wqr, gkv, wk, wvt)


def _dilated_kernel(q_ref, k_ref, v_ref, tbl_ref, o_ref, kpad, vpad, tblv, res, mres, *, seq, q_scale):
    sb = pl.program_id(2)
    t0 = sb * SUPER

    @pl.when(sb == 0)
    def _():
        zeros = jnp.zeros((KV_PAD, LANES), F32)
        kpad[pl.ds(0, KV_PAD), :] = zeros
        vpad[pl.ds(0, KV_PAD), :] = zeros
        kpad[pl.ds(KV_PAD + seq, KV_PAD), :] = zeros
        vpad[pl.ds(KV_PAD + seq, KV_PAD), :] = zeros
        kpad[pl.ds(KV_PAD, seq), :] = k_ref[0]
        vpad[pl.ds(KV_PAD, seq), :] = v_ref[0]
        colk = lax.broadcasted_iota(jnp.int32, (2 * Q_BLOCK, SLAB), 1)
        for gi in range(len(DILATIONS)):
            base = tbl_ref[0, gi]
            tblv[gi, 0] = base
            tblv[gi, 1] = base + jnp.where(colk < HALF, NEG, 0.0)
            tblv[gi, 2] = base + jnp.where(colk >= SLAB - HALF, NEG, 0.0)

    lane_k = lax.broadcasted_iota(jnp.int32, (SLAB, LANES), 1)
    lane_q = lax.broadcasted_iota(jnp.int32, (Q_BLOCK, LANES), 1)

    for gi, r in enumerate(DILATIONS):
        nblk_c = SUPER // (Q_BLOCK * r)
        shift = nblk_c.bit_length() - 1

        def body(i, carry, gi=gi, r=r, nblk_c=nblk_c, shift=shift):
            c = lax.shift_right_logical(i, shift)
            n = i & (nblk_c - 1)
            qstart = c + n * (Q_BLOCK * r)
            kstart = t0 + qstart - HALF * r
            qb = q_ref[0, pl.ds(qstart, Q_BLOCK, stride=r), :] * q_scale
            ks = kpad[pl.ds(KV_PAD + kstart, SLAB, stride=r), :]
            vs = vpad[pl.ds(KV_PAD + kstart, SLAB, stride=r), :]
            q2 = jnp.concatenate([jnp.where(lane_q < HEAD_DIM_A, qb, 0.0),
                                  jnp.where(lane_q < HEAD_DIM_A, 0.0, qb)], axis=0).astype(BF16)
            variant = jnp.where(kstart < 0, 1, jnp.where(kstart + (SLAB - 1) * r >= seq, 2, 0))
            s = _dot_nt(q2, ks.astype(BF16)) + tblv[gi, variant]
            m = jnp.max(s, axis=-1, keepdims=True)
            p = jnp.exp2(s - m).astype(BF16)
            rhs = jnp.concatenate([jnp.where(lane_k < HEAD_DIM_A, vs, 1.0),
                                   jnp.where(lane_k < HEAD_DIM_A, 1.0, vs)], axis=1).astype(BF16)
            o = _dot(p, rhs)
            rows = pl.ds(qstart, Q_BLOCK, stride=r)
            res[gi, 0, rows, :] = o[:Q_BLOCK, :LANES]
            res[gi, 1, rows, :] = o[Q_BLOCK:, LANES:]
            mres[gi, 0, rows, :] = jnp.broadcast_to(m[:Q_BLOCK], (Q_BLOCK, LANES))
            mres[gi, 1, rows, :] = jnp.broadcast_to(m[Q_BLOCK:], (Q_BLOCK, LANES))
            return carry

        lax.fori_loop(0, SUPER // Q_BLOCK, body, 0, unroll=8)

    rows = 256
    lane_o = lax.broadcasted_iota(jnp.int32, (rows, LANES), 1)

    def combine(i, carry):
        sl = pl.ds(i * rows, rows)
        accs = []
        for hh in range(2):
            ms = [mres[gi, hh, sl, :] for gi in range(len(DILATIONS))]
            mx = functools.reduce(jnp.maximum, ms)
            accs.append(sum(jnp.exp2(mm - mx) * res[gi, hh, sl, :] for gi, mm in enumerate(ms)))
        num = jnp.where(lane_o < HEAD_DIM_A, accs[0], accs[1])
        den = pltpu.roll(jnp.where(lane_o < HEAD_DIM_A, accs[1], accs[0]), HEAD_DIM_A, 1)
        o_ref[0, sl, :] = num / den
        return carry

    lax.fori_loop(0, SUPER // rows, combine, 0)


def _dilated_bias_table():
    off = np.abs(np.arange(SLAB)[None, :] - HALF - np.arange(Q_BLOCK)[:, None]).astype(np.float64)
    slopes = 2.0 ** (-8.0 * np.arange(1, N_HEADS + 1) / N_HEADS)
    tbl = np.empty((N_HEADS // 2, len(DILATIONS), 2, Q_BLOCK, SLAB), np.float32)
    for hp in range(N_HEADS // 2):
        for gi, r in enumerate(DILATIONS):
            for hh in range(2):
                bias = -slopes[hp * 2 + hh] * off * r * LOG2E
                tbl[hp, gi, hh] = np.where(off <= HALF, bias, NEG)
    return tbl.reshape(N_HEADS // 2, len(DILATIONS), 2 * Q_BLOCK, SLAB)


def _dilated_attention(pa):
    b, s, _ = pa.shape
    assert s % SUPER == 0 and s // DILATIONS[-1] >= 2 * Q_BLOCK
    nsb = s // SUPER
    npair = N_HEADS // 2
    tbl = jnp.asarray(_dilated_bias_table())
    q_scale = HEAD_DIM_A ** -0.5 * LOG2E
    return pl.pallas_call(
        functools.partial(_dilated_kernel, seq=s, q_scale=q_scale),
        grid=(b, npair, nsb),
        in_specs=[
            pl.BlockSpec((1, SUPER, LANES), lambda bi, hp, sb: (bi, sb, hp)),
            pl.BlockSpec((1, s, LANES), lambda bi, hp, sb: (bi, 0, npair + hp)),
            pl.BlockSpec((1, s, LANES), lambda bi, hp, sb: (bi, 0, 2 * npair + hp)),
            pl.BlockSpec((1, len(DILATIONS), 2 * Q_BLOCK, SLAB), lambda bi, hp, sb: (hp, 0, 0, 0)),
        ],
        out_specs=pl.BlockSpec((1, SUPER, LANES), lambda bi, hp, sb: (bi, sb, hp)),
        out_shape=jax.ShapeDtypeStruct((b, s, WIDTH_A), F32),
        scratch_shapes=[
            pltpu.VMEM((s + 2 * KV_PAD, LANES), F32),
            pltpu.VMEM((s + 2 * KV_PAD, LANES), F32),
            pltpu.VMEM((len(DILATIONS), 3, 2 * Q_BLOCK, SLAB), F32),
            pltpu.VMEM((len(DILATIONS), 2, SUPER, LANES), F32),
            pltpu.VMEM((len(DILATIONS), 2, SUPER, LANES), F32),
        ],
        compiler_params=pltpu.CompilerParams(
            dimension_semantics=("parallel", "parallel", "arbitrary"), vmem_limit_bytes=VMEM_LIMIT),
        name="dilated_attn",
    )(pa, pa, pa, tbl)


def _mla_kernel(q_ref, k_ref, vt_ref, o_ref, s_a, s_b, p_a, p_b, *, tk):
    q = q_ref[0, 0]
    tq = q.shape[0]
    nk = k_ref.shape[2] // tk

    def scores(j, s_buf):
        ks = k_ref[0, 0, pl.ds(pl.multiple_of(j * tk, tk), tk), :]
        s = _dot_nt(ks, q)
        s_buf[...] = s
        return jnp.max(s, axis=0, keepdims=True)

    def softmax(s_buf, p_buf, tmax, m):
        mn = jnp.maximum(m, tmax)
        p_buf[...] = jnp.exp2(s_buf[...] - mn).astype(BF16)
        return mn, jnp.exp2(m - mn)

    ones_rows = jnp.ones((ONES_ROWS, tk), BF16)

    def pv(j, p_buf, alpha, acc):
        return alpha * acc + _dot(jnp.concatenate([vt_ref[0, j], ones_rows], axis=0), p_buf[...])

    def trip(i, carry, last=False):
        t_a, al_b, m, acc = carry
        t_b = scores(2 * i + 1, s_b)
        m, al_a = softmax(s_a, p_a, t_a, m)
        acc = pv(jnp.maximum(2 * i - 1, 0), p_b, al_b, acc)
        if not last:
            t_a = scores(2 * i + 2, s_a)
        m, al_b = softmax(s_b, p_b, t_b, m)
        acc = pv(2 * i, p_a, al_a, acc)
        return t_a, al_b, m, acc

    p_b[...] = jnp.zeros_like(p_b)
    carry = (scores(0, s_a), jnp.ones((1, tq), F32), jnp.full((1, tq), -jnp.inf, F32),
             jnp.zeros((V_DIM + ONES_ROWS, tq), F32))
    carry = lax.fori_loop(0, nk // 2 - 1, trip, carry)
    _, al_b, _, acc = trip(nk // 2 - 1, carry, last=True)
    acc = pv(nk - 1, p_b, al_b, acc)
    o_ref[0] = acc[:V_DIM] / acc[V_DIM:V_DIM + 1]


def _mla_attention(q, k, vt, *, tq):
    b, nh, s, _ = q.shape
    _, nkt, _, tk = vt.shape
    return pl.pallas_call(
        functools.partial(_mla_kernel, tk=tk),
        grid=(b, nh, s // tq),
        in_specs=[
            pl.BlockSpec((1, 1, tq, LANES), lambda bi, h, i: (bi, h, i, 0)),
            pl.BlockSpec((1, 1, s, LANES), lambda bi, h, i: (bi, h, 0, 0)),
            pl.BlockSpec((1, nkt, V_DIM, tk), lambda bi, h, i: (bi, 0, h, 0)),
        ],
        out_specs=pl.BlockSpec((1, V_DIM, tq), lambda bi, h, i: (bi, h, i)),
        out_shape=jax.ShapeDtypeStruct((b, nh * V_DIM, s), F32),
        scratch_shapes=[pltpu.VMEM((tk, tq), F32), pltpu.VMEM((tk, tq), F32),
                        pltpu.VMEM((tk, tq), BF16), pltpu.VMEM((tk, tq), BF16)],
        compiler_params=pltpu.CompilerParams(
            dimension_semantics=("parallel", "parallel", "arbitrary"), vmem_limit_bytes=VMEM_LIMIT),
        name="mla_attn",
    )(q, k, vt)


def _outproj_kernel(ya_ref, ybt_ref, x_ref, ga_ref, gb_ref, wo_ref, gp_ref, o_ref):
    yan = _rms(ya_ref[0], ga_ref[...]).astype(BF16)
    ybn = _rms(ybt_ref[0].T, gb_ref[...]).astype(BF16)
    y = _dot(yan, wo_ref[:WIDTH_A, :]) + _dot(ybn, wo_ref[WIDTH_A:, :])
    o_ref[0] = x_ref[0] + _rms(y, gp_ref[...])


def _outproj(ya, ybt, x, ga, gb, wo, gp, *, tm):
    b, s, d = x.shape
    full = lambda a: pl.BlockSpec(a.shape, lambda bi, i: (0,) * a.ndim)
    return pl.pallas_call(
        _outproj_kernel,
        grid=(b, s // tm),
        in_specs=[
            pl.BlockSpec((1, tm, WIDTH_A), lambda bi, i: (bi, i, 0)),
            pl.BlockSpec((1, WIDTH_B, tm), lambda bi, i: (bi, 0, i)),
            pl.BlockSpec((1, tm, d), lambda bi, i: (bi, i, 0)),
            full(ga), full(gb), full(wo), full(gp),
        ],
        out_specs=pl.BlockSpec((1, tm, d), lambda bi, i: (bi, i, 0)),
        out_shape=jax.ShapeDtypeStruct((b, s, d), F32),
        compiler_params=pltpu.CompilerParams(
            dimension_semantics=("parallel", "parallel"), vmem_limit_bytes=VMEM_LIMIT),
        name="outproj",
    )(ya, ybt, x, ga, gb, wo, gp)


FF_GROUPS = ((0, 4), (4, 8), (8, 11))


def _ffn_kernel(taps_ref, xp_ref, x_ref, xn_ref, g_ref, wup_ref, cw_ref, cb_ref, wd_ref, gp_ref, o_ref,
                ubuf, abuf, *, tm, nc):
    i = pl.program_id(1)
    g = g_ref[...]
    hp = _rms(xp_ref[0], g) * jnp.where(i > 0, 1.0, 0.0)
    hn = _rms(xn_ref[0], g) * jnp.where(i < pl.num_programs(1) - 1, 1.0, 0.0)
    h = jnp.concatenate([hp, _rms(x_ref[0], g), hn], axis=0).astype(BF16)
    nslab = FF_CHUNK // LANES

    def up(j, slot):
        for part in range(2):
            u = _dot(h, wup_ref[part * nc + j])
            for t in range(nslab):
                ubuf[slot, part, t] = u[:, t * LANES:(t + 1) * LANES]

    def conv(slot, part, t, c):
        cw = cw_ref[c]
        acc = cb_ref[c][:, t * LANES:(t + 1) * LANES]
        for tap in range(3):
            rows = ubuf[slot, part, t, pl.ds(taps_ref[tap], tm, stride=1), :]
            acc = acc + rows * cw[tap:tap + 1, t * LANES:(t + 1) * LANES]
        return acc

    def gate(j, slot):
        for t in range(nslab):
            a = jax.nn.gelu(conv(slot, 0, t, j), approximate=True) * conv(slot, 1, t, nc + j)
            abuf[:, j * FF_CHUNK + t * LANES:j * FF_CHUNK + (t + 1) * LANES] = a.astype(BF16)

    up(0, 0)
    y = None
    for j in range(nc):
        if j + 1 < nc:
            up(j + 1, (j + 1) % 2)
        gate(j, j % 2)
        for lo, hi in FF_GROUPS:
            if j + 1 == hi:
                part = _dot(abuf[:, lo * FF_CHUNK:hi * FF_CHUNK], wd_ref[lo * FF_CHUNK:hi * FF_CHUNK, :])
                y = part if y is None else y + part
    o_ref[0] = x_ref[0] + _rms(y, gp_ref[...])


def _ffn(x, g, wup, cw, cb, wd, gp, *, tm):
    b, s, d = x.shape
    nt = s // tm
    nc = wup.shape[0] // 2
    hb = tm // HALO
    last = s // HALO - 1
    taps = jnp.arange(HALO - 1, HALO + 2, dtype=jnp.int32)
    resident = lambda a: pl.BlockSpec(a.shape, lambda bi, i: (0,) * a.ndim, pipeline_mode=pl.Buffered(1))
    return pl.pallas_call(
        functools.partial(_ffn_kernel, tm=tm, nc=nc),
        grid=(b, nt),
        in_specs=[
            pl.BlockSpec(memory_space=pltpu.SMEM),
            pl.BlockSpec((1, HALO, d), lambda bi, i: (bi, jnp.maximum(i * hb - 1, 0), 0)),
            pl.BlockSpec((1, tm, d), lambda bi, i: (bi, i, 0)),
            pl.BlockSpec((1, HALO, d), lambda bi, i: (bi, jnp.minimum((i + 1) * hb, last), 0)),
            resident(g), resident(wup), resident(cw), resident(cb), resident(wd), resident(gp),
        ],
        out_specs=pl.BlockSpec((1, tm, d), lambda bi, i: (bi, i, 0)),
        out_shape=jax.ShapeDtypeStruct((b, s, d), F32),
        scratch_shapes=[
            pltpu.VMEM((2, 2, FF_CHUNK // LANES, tm + 2 * HALO, LANES), F32),
            pltpu.VMEM((tm, D_FF), BF16),
        ],
        compiler_params=pltpu.CompilerParams(
            dimension_semantics=("parallel", "parallel"), vmem_limit_bytes=VMEM_LIMIT),
        name="convffn",
    )(taps, x, x, x, g, wup, cw, cb, wd, gp)


def _rot_cols(w):
    half = w.shape[-1] // 2
    return jnp.concatenate([-w[..., half:], w[..., :half]], axis=-1)


def _pad_cols(w, before, total):
    return jnp.pad(w, [(0, 0)] * (w.ndim - 1) + [(before, total - before - w.shape[-1])])


def _layer(x, norm_mix_pre, w_in, q_lat_norm, w_uq, kv_lat_norm, w_ukv, out_norm_a, out_norm_b, w_o,
           norm_mix_post, norm_ffn_pre, w_up, conv_w, conv_b, w_down, norm_ffn_post):
    b, s, d = x.shape
    row = lambda v: v.reshape(1, -1).astype(F32)

    c0 = 3 * WIDTH_A + Q_LORA + KV_LORA
    w_kr = w_in[:, c0:]
    win = jnp.concatenate(
        [w_in[:, :c0], _pad_cols(w_kr, QK_NOPE, LANES), _pad_cols(_rot_cols(w_kr), QK_NOPE, LANES)],
        axis=1).astype(BF16)
    wq3 = w_uq.reshape(Q_LORA, N_HEADS, QK_NOPE + QK_ROPE)
    wq = _pad_cols(wq3, 0, LANES).reshape(Q_LORA, N_HEADS * LANES).astype(BF16)
    wqr = _pad_cols(_rot_cols(wq3[..., QK_NOPE:]), QK_NOPE, LANES).reshape(Q_LORA, N_HEADS * LANES).astype(BF16)
    wkv3 = w_ukv.reshape(KV_LORA, N_HEADS, QK_NOPE + V_DIM)
    wk = _pad_cols(wkv3[..., :QK_NOPE], 0, LANES).reshape(KV_LORA, N_HEADS * LANES).astype(BF16)
    wvt = wkv3[..., QK_NOPE:].reshape(KV_LORA, WIDTH_B).T.astype(BF16)

    pos = jnp.arange(s, dtype=F32)
    inv_freq = jnp.exp(-math.log(ROPE_BASE) * jnp.arange(0, QK_ROPE, 2, dtype=F32) / QK_ROPE)
    ang = pos[:, None] * inv_freq[None, :]
    cosp = _pad_cols(jnp.tile(jnp.cos(ang), (1, 2)), QK_NOPE, LANES)
    sinp = _pad_cols(jnp.tile(jnp.sin(ang), (1, 2)), QK_NOPE, LANES)

    q_scale = (QK_NOPE + QK_ROPE) ** -0.5 * LOG2E
    pa, q, k, vt = _inproj(x, row(norm_mix_pre), win, cosp, sinp, row(q_lat_norm), wq, wqr,
                           row(kv_lat_norm), wk, wvt, tm=512, q_scale=q_scale)
    ya = _dilated_attention(pa)
    ybt = _mla_attention(q, k, vt, tq=4096)
    x1 = _outproj(ya, ybt, x, row(out_norm_a), row(out_norm_b), w_o.astype(BF16), row(norm_mix_post), tm=512)
    n2 = 2 * D_FF // FF_CHUNK
    wup = w_up.astype(BF16).reshape(d, n2, FF_CHUNK).transpose(1, 0, 2)
    cw = conv_w.reshape(3, n2, FF_CHUNK).transpose(1, 0, 2)
    cb = conv_b.reshape(n2, 1, FF_CHUNK)
    return _ffn(x1, row(norm_ffn_pre), wup, cw, cb, w_down.astype(BF16), row(norm_ffn_post), tm=512)


def kernel(x, norm_mix_pre, w_in, q_lat_norm, w_uq, kv_lat_norm, w_ukv, out_norm_a, out_norm_b, w_o,
           norm_mix_post, norm_ffn_pre, w_up, conv_w, conv_b, w_down, norm_ffn_post):
    for l in range(norm_mix_pre.shape[0]):
        x = _layer(x, norm_mix_pre[l], w_in[l], q_lat_norm[l], w_uq[l], kv_lat_norm[l], w_ukv[l],
                   out_norm_a[l], out_norm_b[l], w_o[l], norm_mix_post[l], norm_ffn_pre[l], w_up[l],
                   conv_w[l], conv_b[l], w_down[l], norm_ffn_post[l])
    return x
```

```python
import functools
import math

import numpy as np
import jax
import jax.numpy as jnp
from jax import lax
from jax.experimental import pallas as pl
from jax.experimental.pallas import tpu as pltpu

F32 = jnp.float32
BF16 = jnp.bfloat16

EPS = 1e-6
NEG = -1e30
LOG2E = 1.4426950408889634

D_MODEL = 1024
N_HEADS = 8
HEAD_DIM_A = 64
WIDTH_A = N_HEADS * HEAD_DIM_A
DILATIONS = (1, 4, 16)
HALF = 64
Q_BLOCK = 128
SLAB = Q_BLOCK + 2 * HALF
SUPER = Q_BLOCK * DILATIONS[-1]
KV_PAD = HALF * DILATIONS[-1]
QK_NOPE = 64
QK_ROPE = 32
V_DIM = 64
WIDTH_B = N_HEADS * V_DIM
Q_LORA = 384
KV_LORA = 256
ROPE_BASE = 10000.0
D_FF = 2816
FF_CHUNK = 256
LANES = 128
HALO = 16
ONES_ROWS = 16

VMEM_LIMIT = 56 * 1024 * 1024


def _rms(x, g):
    return x * lax.rsqrt(jnp.mean(x * x, axis=-1, keepdims=True) + EPS) * g


def _dot(a, b):
    return jnp.dot(a, b, preferred_element_type=F32)


def _dot_nt(a, b):
    return lax.dot_general(a, b, (((1,), (1,)), ((), ())), preferred_element_type=F32)


ROT_SHIFT = LANES - QK_ROPE


def _inproj_kernel(x_ref, g_ref, win_ref, cos_ref, sin_ref, gq_ref, wq_ref, gkv_ref, wk_ref,
                   wvt_ref, pa_ref, q_ref, k_ref, vt_ref, *, q_scale):
    h = _rms(x_ref[0], g_ref[...]).astype(BF16)
    pa_ref[0] = _dot(h, win_ref[:, :3 * WIDTH_A])
    lat = _dot(h, win_ref[:, 3 * WIDTH_A:])
    c_q = lat[:, :Q_LORA]
    c_kv = lat[:, Q_LORA:Q_LORA + KV_LORA]
    kr = lat[:, Q_LORA + KV_LORA:]
    cosp = cos_ref[...]
    sinp = sin_ref[...]
    lane = lax.broadcasted_iota(jnp.int32, cosp.shape, 1)
    cq_tab = (cosp + jnp.where(lane < QK_NOPE, 1.0, 0.0)) * q_scale
    sq_tab = sinp * q_scale

    cqn = _rms(c_q, gq_ref[...]).astype(BF16)
    qf = _dot(cqn, wq_ref[...])
    for hh in range(N_HEADS):
        qh = qf[:, hh * LANES:(hh + 1) * LANES]
        q_ref[0, hh] = (qh * cq_tab + pltpu.roll(qh, ROT_SHIFT, 1) * sq_tab).astype(BF16)

    ckvn = _rms(c_kv, gkv_ref[...]).astype(BF16)
    kf = _dot(ckvn, wk_ref[...])
    krope = kr * cosp + pltpu.roll(kr, ROT_SHIFT, 1) * sinp
    for hh in range(N_HEADS):
        sl = slice(hh * LANES, (hh + 1) * LANES)
        k_ref[0, hh] = (kf[:, sl] + krope).astype(BF16)
    vt_ref[0, 0] = _dot_nt(wvt_ref[...], ckvn).astype(BF16)


def _inproj(x, g, win, cosp, sinp, gq, wq, gkv, wk, wvt, *, tm, q_scale):
    b, s, d = x.shape
    nt = s // tm
    full = lambda a: pl.BlockSpec(a.shape, lambda bi, i: (0,) * a.ndim)
    return pl.pallas_call(
        functools.partial(_inproj_kernel, q_scale=q_scale),
        grid=(b, nt),
        in_specs=[
            pl.BlockSpec((1, tm, d), lambda bi, i: (bi, i, 0)),
            full(g), full(win),
            pl.BlockSpec((tm, LANES), lambda bi, i: (i, 0)),
            pl.BlockSpec((tm, LANES), lambda bi, i: (i, 0)),
            full(gq), full(wq), full(gkv), full(wk), full(wvt),
        ],
        out_specs=[
            pl.BlockSpec((1, tm, 3 * WIDTH_A), lambda bi, i: (bi, i, 0)),
            pl.BlockSpec((1, N_HEADS, tm, LANES), lambda bi, i: (bi, 0, i, 0)),
            pl.BlockSpec((1, N_HEADS, tm, LANES), lambda bi, i: (bi, 0, i, 0)),
            pl.BlockSpec((1, 1, WIDTH_B, tm), lambda bi, i: (bi, i, 0, 0)),
        ],
        out_shape=[
            jax.ShapeDtypeStruct((b, s, 3 * WIDTH_A), F32),
            jax.ShapeDtypeStruct((b, N_HEADS, s, LANES), BF16),
            jax.ShapeDtypeStruct((b, N_HEADS, s, LANES), BF16),
            jax.ShapeDtypeStruct((b, nt, WIDTH_B, tm), BF16),
        ],
        compiler_params=pltpu.CompilerParams(
            dimension_semantics=("parallel", "parallel"), vmem_limit_bytes=VMEM_LIMIT),
        name="inproj",
    )(x, g, win, cosp, sinp, gq, wq, gkv, wk, wvt)


RES_O, RES_L, RES_M = 0, 1, 2
R_FAR = DILATIONS[-1]
R_MID = 4


def _dilated_kernel(q_ref, k_ref, v_ref, tbl_ref, o_ref, kpad, vpad, tblv, tmp, kfar, vfar, res, *, seq, q_scale):
    sb = pl.program_id(2)
    t0 = sb * SUPER
    lp_far = seq // R_FAR + 2 * HALF

    @pl.when(sb == 0)
    def _():
        zeros = jnp.zeros((KV_PAD, LANES), F32)
        kpad[pl.ds(0, KV_PAD), :] = zeros
        vpad[pl.ds(0, KV_PAD), :] = zeros
        kpad[pl.ds(KV_PAD + seq, KV_PAD), :] = zeros
        vpad[pl.ds(KV_PAD + seq, KV_PAD), :] = zeros
        kpad[pl.ds(KV_PAD, seq), :] = k_ref[0]
        vpad[pl.ds(KV_PAD, seq), :] = v_ref[0]
        colk = lax.broadcasted_iota(jnp.int32, (2 * Q_BLOCK, SLAB), 1)
        for gi in range(len(DILATIONS)):
            base = tbl_ref[0, gi]
            tblv[gi, 0] = base
            tblv[gi, 1] = base + jnp.where(colk < HALF, NEG, 0.0)
            tblv[gi, 2] = base + jnp.where(colk >= SLAB - HALF, NEG, 0.0)
        sub = R_FAR // R_MID
        for src, dst in ((kpad, kfar), (vpad, vfar)):
            for c_lo in range(R_MID):
                tmp[...] = src[pl.ds(c_lo, sub * lp_far, stride=R_MID), :]
                for c_hi in range(sub):
                    dst[pl.ds((c_lo + R_MID * c_hi) * lp_far, lp_far), :] = (
                        tmp[pl.ds(c_hi, lp_far, stride=sub), :].astype(BF16))

    lane_k = lax.broadcasted_iota(jnp.int32, (SLAB, LANES), 1)
    lane_q = lax.broadcasted_iota(jnp.int32, (Q_BLOCK, LANES), 1)

    for gi, r in enumerate(DILATIONS):
        nblk_c = SUPER // (Q_BLOCK * r)
        shift = nblk_c.bit_length() - 1

        def body(i, carry, gi=gi, r=r, nblk_c=nblk_c, shift=shift):
            c = lax.shift_right_logical(i, shift)
            n = i & (nblk_c - 1)
            qstart = c + n * (Q_BLOCK * r)
            kstart = t0 + qstart - HALF * r
            qb = q_ref[0, pl.ds(qstart, Q_BLOCK, stride=r), :] * q_scale
            if r == R_FAR:
                row0 = pl.multiple_of(c * lp_far + (sb * nblk_c + n) * Q_BLOCK, Q_BLOCK)
                ks = kfar[pl.ds(row0, SLAB), :]
                vs = vfar[pl.ds(row0, SLAB), :]
            else:
                ks = kpad[pl.ds(KV_PAD + kstart, SLAB, stride=r), :].astype(BF16)
                vs = vpad[pl.ds(KV_PAD + kstart, SLAB, stride=r), :]
            q2 = jnp.concatenate([jnp.where(lane_q < HEAD_DIM_A, qb, 0.0),
                                  jnp.where(lane_q < HEAD_DIM_A, 0.0, qb)], axis=0).astype(BF16)
            variant = jnp.where(kstart < 0, 1, jnp.where(kstart + (SLAB - 1) * r >= seq, 2, 0))
            s = _dot_nt(q2, ks) + tblv[gi, variant]
            m = jnp.max(s, axis=-1, keepdims=True)
            p = jnp.exp2(s - m).astype(BF16)
            one = jnp.ones((), vs.dtype)
            rhs = jnp.concatenate([jnp.where(lane_k < HEAD_DIM_A, vs, one),
                                   jnp.where(lane_k < HEAD_DIM_A, one, vs)], axis=1).astype(BF16)
            o = _dot(p, rhs)
            rows = pl.ds(qstart, Q_BLOCK, stride=r)
            head0 = lane_q < HEAD_DIM_A
            res[gi, RES_O, rows, :] = jnp.where(head0, o[:Q_BLOCK, :LANES], o[Q_BLOCK:, LANES:])
            res[gi, RES_L, rows, :] = jnp.where(head0, o[:Q_BLOCK, LANES:], o[Q_BLOCK:, :LANES])
            res[gi, RES_M, rows, :] = jnp.where(head0, m[:Q_BLOCK], m[Q_BLOCK:])
            return carry

        lax.fori_loop(0, SUPER // Q_BLOCK, body, 0, unroll=8)

    rows = 256

    def combine(i, carry):
        sl = pl.ds(i * rows, rows)
        ms = [res[gi, RES_M, sl, :] for gi in range(len(DILATIONS))]
        mx = functools.reduce(jnp.maximum, ms)
        ws = [jnp.exp2(mm - mx) for mm in ms]
        num = sum(w * res[gi, RES_O, sl, :] for gi, w in enumerate(ws))
        den = sum(w * res[gi, RES_L, sl, :] for gi, w in enumerate(ws))
        o_ref[0, sl, :] = num / den
        return carry

    lax.fori_loop(0, SUPER // rows, combine, 0)


def _dilated_bias_table():
    off = np.abs(np.arange(SLAB)[None, :] - HALF - np.arange(Q_BLOCK)[:, None]).astype(np.float64)
    slopes = 2.0 ** (-8.0 * np.arange(1, N_HEADS + 1) / N_HEADS)
    tbl = np.empty((N_HEADS // 2, len(DILATIONS), 2, Q_BLOCK, SLAB), np.float32)
    for hp in range(N_HEADS // 2):
        for gi, r in enumerate(DILATIONS):
            for hh in range(2):
                bias = -slopes[hp * 2 + hh] * off * r * LOG2E
                tbl[hp, gi, hh] = np.where(off <= HALF, bias, NEG)
    return tbl.reshape(N_HEADS // 2, len(DILATIONS), 2 * Q_BLOCK, SLAB)


def _dilated_attention(pa):
    b, s, _ = pa.shape
    assert s % SUPER == 0 and s // DILATIONS[-1] >= 2 * Q_BLOCK
    nsb = s // SUPER
    npair = N_HEADS // 2
    tbl = jnp.asarray(_dilated_bias_table())
    q_scale = HEAD_DIM_A ** -0.5 * LOG2E
    return pl.pallas_call(
        functools.partial(_dilated_kernel, seq=s, q_scale=q_scale),
        grid=(b, npair, nsb),
        in_specs=[
            pl.BlockSpec((1, SUPER, LANES), lambda bi, hp, sb: (bi, sb, hp)),
            pl.BlockSpec((1, s, LANES), lambda bi, hp, sb: (bi, 0, npair + hp)),
            pl.BlockSpec((1, s, LANES), lambda bi, hp, sb: (bi, 0, 2 * npair + hp)),
            pl.BlockSpec((1, len(DILATIONS), 2 * Q_BLOCK, SLAB), lambda bi, hp, sb: (hp, 0, 0, 0)),
        ],
        out_specs=pl.BlockSpec((1, SUPER, LANES), lambda bi, hp, sb: (bi, sb, hp)),
        out_shape=jax.ShapeDtypeStruct((b, s, WIDTH_A), F32),
        scratch_shapes=[
            pltpu.VMEM((s + 2 * KV_PAD, LANES), F32),
            pltpu.VMEM((s + 2 * KV_PAD, LANES), F32),
            pltpu.VMEM((len(DILATIONS), 3, 2 * Q_BLOCK, SLAB), F32),
            pltpu.VMEM((s // R_MID + 2 * HALF * (R_FAR // R_MID), LANES), F32),
            pltpu.VMEM((s + 2 * HALF * R_FAR, LANES), BF16),
            pltpu.VMEM((s + 2 * HALF * R_FAR, LANES), BF16),
            pltpu.VMEM((len(DILATIONS), 3, SUPER, LANES), F32),
        ],
        compiler_params=pltpu.CompilerParams(
            dimension_semantics=("parallel", "parallel", "arbitrary"), vmem_limit_bytes=VMEM_LIMIT),
        name="dilated_attn",
    )(pa, pa, pa, tbl)


def _mla_kernel(q_ref, k_ref, vt_ref, o_ref, s_a, s_b, p_a, p_b, *, tk):
    q = q_ref[0, 0]
    tq = q.shape[0]
    nk = k_ref.shape[2] // tk

    def scores(j, s_buf):
        ks = k_ref[0, 0, pl.ds(pl.multiple_of(j * tk, tk), tk), :]
        s = _dot_nt(ks, q)
        s_buf[...] = s
        return jnp.max(s, axis=0, keepdims=True)

    def softmax(s_buf, p_buf, tmax, m):
        mn = jnp.maximum(m, tmax)
        p_buf[...] = jnp.exp2(s_buf[...] - mn).astype(BF16)
        return mn, jnp.exp2(m - mn)

    ones_rows = jnp.ones((ONES_ROWS, tk), BF16)

    def pv(j, p_buf, alpha, acc):
        return alpha * acc + _dot(jnp.concatenate([vt_ref[0, j], ones_rows], axis=0), p_buf[...])

    def trip(i, carry, last=False):
        t_a, al_b, m, acc = carry
        t_b = scores(2 * i + 1, s_b)
        m, al_a = softmax(s_a, p_a, t_a, m)
        acc = pv(jnp.maximum(2 * i - 1, 0), p_b, al_b, acc)
        if not last:
            t_a = scores(2 * i + 2, s_a)
        m, al_b = softmax(s_b, p_b, t_b, m)
        acc = pv(2 * i, p_a, al_a, acc)
        return t_a, al_b, m, acc

    p_b[...] = jnp.zeros_like(p_b)
    carry = (scores(0, s_a), jnp.ones((1, tq), F32), jnp.full((1, tq), -jnp.inf, F32),
             jnp.zeros((V_DIM + ONES_ROWS, tq), F32))
    carry = lax.fori_loop(0, nk // 2 - 1, trip, carry)
    _, al_b, _, acc = trip(nk // 2 - 1, carry, last=True)
    acc = pv(nk - 1, p_b, al_b, acc)
    o_ref[0] = acc[:V_DIM] / acc[V_DIM:V_DIM + 1]


def _mla_attention(q, k, vt, *, tq):
    b, nh, s, _ = q.shape
    _, nkt, _, tk = vt.shape
    return pl.pallas_call(
        functools.partial(_mla_kernel, tk=tk),
        grid=(b, nh, s // tq),
        in_specs=[
            pl.BlockSpec((1, 1, tq, LANES), lambda bi, h, i: (bi, h, i, 0)),
            pl.BlockSpec((1, 1, s, LANES), lambda bi, h, i: (bi, h, 0, 0)),
            pl.BlockSpec((1, nkt, V_DIM, tk), lambda bi, h, i: (bi, 0, h, 0)),
        ],
        out_specs=pl.BlockSpec((1, V_DIM, tq), lambda bi, h, i: (bi, h, i)),
        out_shape=jax.ShapeDtypeStruct((b, nh * V_DIM, s), F32),
        scratch_shapes=[pltpu.VMEM((tk, tq), F32), pltpu.VMEM((tk, tq), F32),
                        pltpu.VMEM((tk, tq), BF16), pltpu.VMEM((tk, tq), BF16)],
        compiler_params=pltpu.CompilerParams(
            dimension_semantics=("parallel", "parallel", "arbitrary"), vmem_limit_bytes=VMEM_LIMIT),
        name="mla_attn",
    )(q, k, vt)


def _outproj_kernel(ya_ref, ybt_ref, x_ref, ga_ref, gb_ref, wo_ref, gp_ref, o_ref):
    yan = _rms(ya_ref[0], ga_ref[...]).astype(BF16)
    ybn = _rms(ybt_ref[0].T, gb_ref[...]).astype(BF16)
    y = _dot(yan, wo_ref[:WIDTH_A, :]) + _dot(ybn, wo_ref[WIDTH_A:, :])
    o_ref[0] = x_ref[0] + _rms(y, gp_ref[...])


def _outproj(ya, ybt, x, ga, gb, wo, gp, *, tm):
    b, s, d = x.shape
    full = lambda a: pl.BlockSpec(a.shape, lambda bi, i: (0,) * a.ndim)
    return pl.pallas_call(
        _outproj_kernel,
        grid=(b, s // tm),
        in_specs=[
            pl.BlockSpec((1, tm, WIDTH_A), lambda bi, i: (bi, i, 0)),
            pl.BlockSpec((1, WIDTH_B, tm), lambda bi, i: (bi, 0, i)),
            pl.BlockSpec((1, tm, d), lambda bi, i: (bi, i, 0)),
            full(ga), full(gb), full(wo), full(gp),
        ],
        out_specs=pl.BlockSpec((1, tm, d), lambda bi, i: (bi, i, 0)),
        out_shape=jax.ShapeDtypeStruct((b, s, d), F32),
        compiler_params=pltpu.CompilerParams(
            dimension_semantics=("parallel", "parallel"), vmem_limit_bytes=VMEM_LIMIT),
        name="outproj",
    )(ya, ybt, x, ga, gb, wo, gp)


FF_GROUPS = ((0, 4), (4, 8), (8, 11))


def _ffn_kernel(taps_ref, xp_ref, x_ref, xn_ref, g_ref, wup_ref, cw_ref, cb_ref, wd_ref, gp_ref, o_ref,
                ubuf, abuf, *, tm):
    i = pl.program_id(1)
    g = g_ref[...]
    hp = _rms(xp_ref[0], g) * jnp.where(i > 0, 1.0, 0.0)
    hn = _rms(xn_ref[0], g) * jnp.where(i < pl.num_programs(1) - 1, 1.0, 0.0)
    h = jnp.concatenate([hp, _rms(x_ref[0], g), hn], axis=0).astype(BF16)
    nslab = FF_CHUNK // LANES
    nc = D_FF // FF_CHUNK

    def up(j, slot):
        for part in range(2):
            c0 = part * D_FF + j * FF_CHUNK
            u = _dot(h, wup_ref[:, c0:c0 + FF_CHUNK])
            for t in range(nslab):
                ubuf[slot, part, t] = u[:, t * LANES:(t + 1) * LANES]

    def conv(slot, part, j, t):
        c0 = part * D_FF + j * FF_CHUNK + t * LANES
        cw = cw_ref[:, c0:c0 + LANES]
        acc = cb_ref[:, c0:c0 + LANES]
        for tap in range(3):
            acc = acc + ubuf[slot, part, t, pl.ds(taps_ref[tap], tm, stride=1), :] * cw[tap:tap + 1]
        return acc

    def gate(j, slot):
        for t in range(nslab):
            a = jax.nn.gelu(conv(slot, 0, j, t), approximate=True) * conv(slot, 1, j, t)
            abuf[:, j * FF_CHUNK + t * LANES:j * FF_CHUNK + (t + 1) * LANES] = a.astype(BF16)

    up(0, 0)
    y = None
    for j in range(nc):
        if j + 1 < nc:
            up(j + 1, (j + 1) % 2)
        gate(j, j % 2)
        for lo, hi in FF_GROUPS:
            if j + 1 == hi:
                part = _dot(abuf[:, lo * FF_CHUNK:hi * FF_CHUNK], wd_ref[lo * FF_CHUNK:hi * FF_CHUNK, :])
                y = part if y is None else y + part
    o_ref[0] = x_ref[0] + _rms(y, gp_ref[...])


def _ffn(x, g, wup, cw, cb, wd, gp, *, tm):
    b, s, d = x.shape
    nt = s // tm
    hb = tm // HALO
    last = s // HALO - 1
    taps = jnp.arange(HALO - 1, HALO + 2, dtype=jnp.int32)
    resident = lambda a: pl.BlockSpec(a.shape, lambda bi, i: (0,) * a.ndim, pipeline_mode=pl.Buffered(1))
    return pl.pallas_call(
        functools.partial(_ffn_kernel, tm=tm),
        grid=(b, nt),
        in_specs=[
            pl.BlockSpec(memory_space=pltpu.SMEM),
            pl.BlockSpec((1, HALO, d), lambda bi, i: (bi, jnp.maximum(i * hb - 1, 0), 0)),
            pl.BlockSpec((1, tm, d), lambda bi, i: (bi, i, 0)),
            pl.BlockSpec((1, HALO, d), lambda bi, i: (bi, jnp.minimum((i + 1) * hb, last), 0)),
            resident(g), resident(wup), resident(cw), resident(cb), resident(wd), resident(gp),
        ],
        out_specs=pl.BlockSpec((1, tm, d), lambda bi, i: (bi, i, 0)),
        out_shape=jax.ShapeDtypeStruct((b, s, d), F32),
        scratch_shapes=[
            pltpu.VMEM((2, 2, FF_CHUNK // LANES, tm + 2 * HALO, LANES), F32),
            pltpu.VMEM((tm, D_FF), BF16),
        ],
        compiler_params=pltpu.CompilerParams(
            dimension_semantics=("parallel", "parallel"), vmem_limit_bytes=VMEM_LIMIT),
        name="convffn",
    )(taps, x, x, x, g, wup, cw, cb, wd, gp)


def _rot_cols(w):
    half = w.shape[-1] // 2
    return jnp.concatenate([-w[..., half:], w[..., :half]], axis=-1)


def _pad_cols(w, before, total):
    return jnp.pad(w, [(0, 0)] * (w.ndim - 1) + [(before, total - before - w.shape[-1])])


def _layer(x, norm_mix_pre, w_in, q_lat_norm, w_uq, kv_lat_norm, w_ukv, out_norm_a, out_norm_b, w_o,
           norm_mix_post, norm_ffn_pre, w_up, conv_w, conv_b, w_down, norm_ffn_post):
    b, s, d = x.shape
    row = lambda v: v.reshape(1, -1).astype(F32)

    c0 = 3 * WIDTH_A + Q_LORA + KV_LORA
    w_kr = w_in[:, c0:]
    win = jnp.concatenate(
        [w_in[:, :c0], _pad_cols(jnp.concatenate([w_kr, _rot_cols(w_kr)], axis=1), QK_NOPE, LANES)],
        axis=1).astype(BF16)
    wq3 = w_uq.reshape(Q_LORA, N_HEADS, QK_NOPE + QK_ROPE)
    wq = jnp.concatenate([wq3, _rot_cols(wq3[..., QK_NOPE:])], axis=-1).reshape(
        Q_LORA, N_HEADS * LANES).astype(BF16)
    wkv3 = w_ukv.reshape(KV_LORA, N_HEADS, QK_NOPE + V_DIM)
    wk = _pad_cols(wkv3[..., :QK_NOPE], 0, LANES).reshape(KV_LORA, N_HEADS * LANES).astype(BF16)
    wvt = wkv3[..., QK_NOPE:].reshape(KV_LORA, WIDTH_B).T.astype(BF16)

    pos = jnp.arange(s, dtype=F32)
    inv_freq = jnp.exp(-math.log(ROPE_BASE) * jnp.arange(0, QK_ROPE, 2, dtype=F32) / QK_ROPE)
    ang = pos[:, None] * inv_freq[None, :]
    cosp = _pad_cols(jnp.tile(jnp.cos(ang), (1, 2)), QK_NOPE, LANES)
    sinp = _pad_cols(jnp.tile(jnp.sin(ang), (1, 2)), QK_NOPE, LANES)

    q_scale = (QK_NOPE + QK_ROPE) ** -0.5 * LOG2E
    pa, q, k, vt = _inproj(x, row(norm_mix_pre), win, cosp, sinp, row(q_lat_norm), wq,
                           row(kv_lat_norm), wk, wvt, tm=512, q_scale=q_scale)
    ya = _dilated_attention(pa)
    ybt = _mla_attention(q, k, vt, tq=4096)
    x1 = _outproj(ya, ybt, x, row(out_norm_a), row(out_norm_b), w_o.astype(BF16), row(norm_mix_post), tm=512)
    return _ffn(x1, row(norm_ffn_pre), w_up.astype(BF16), conv_w, conv_b.reshape(1, -1),
                w_down.astype(BF16), row(norm_ffn_post), tm=512)


def kernel(x, norm_mix_pre, w_in, q_lat_norm, w_uq, kv_lat_norm, w_ukv, out_norm_a, out_norm_b, w_o,
           norm_mix_post, norm_ffn_pre, w_up, conv_w, conv_b, w_down, norm_ffn_post):
    for l in range(norm_mix_pre.shape[0]):
        x = _layer(x, norm_mix_pre[l], w_in[l], q_lat_norm[l], w_uq[l], kv_lat_norm[l], w_ukv[l],
                   out_norm_a[l], out_norm_b[l], w_o[l], norm_mix_post[l], norm_ffn_pre[l], w_up[l],
                   conv_w[l], conv_b[l], w_down[l], norm_ffn_post[l])
    return x
```

```python
import functools
import math

import numpy as np
import jax
import jax.numpy as jnp
from jax import lax
from jax.experimental import pallas as pl
from jax.experimental.pallas import tpu as pltpu

F32 = jnp.float32
BF16 = jnp.bfloat16

EPS = 1e-6
NEG = -1e30
LOG2E = 1.4426950408889634

D_MODEL = 1024
N_HEADS = 8
HEAD_DIM_A = 64
WIDTH_A = N_HEADS * HEAD_DIM_A
DILATIONS = (1, 4, 16)
HALF = 64
Q_BLOCK = 128
SLAB = Q_BLOCK + 2 * HALF
SUPER = Q_BLOCK * DILATIONS[-1]
KV_PAD = HALF * DILATIONS[-1]
QK_NOPE = 64
QK_ROPE = 32
V_DIM = 64
WIDTH_B = N_HEADS * V_DIM
Q_LORA = 384
KV_LORA = 256
ROPE_BASE = 10000.0
D_FF = 2816
FF_CHUNK = 256
LANES = 128
HALO = 16
ONES_ROWS = 16

VMEM_LIMIT = 56 * 1024 * 1024


def _rms(x, g):
    return x * lax.rsqrt(jnp.mean(x * x, axis=-1, keepdims=True) + EPS) * g


def _dot(a, b):
    return jnp.dot(a, b, preferred_element_type=F32)


def _dot_nt(a, b):
    return lax.dot_general(a, b, (((1,), (1,)), ((), ())), preferred_element_type=F32)


ROT_SHIFT = LANES - QK_ROPE
SHIFT_LANE = QK_NOPE + QK_ROPE
BOUND_SLACK = 1.002
MAX_SHIFT = 60.0


def _inproj_kernel(x_ref, g_ref, win_ref, cos_ref, sin_ref, gq_ref, wq_ref, gkv_ref, wk_ref,
                   wvt_ref, pa_ref, q_ref, k_ref, vt_ref, *, q_scale):
    h = _rms(x_ref[0], g_ref[...]).astype(BF16)
    pa_ref[0] = _dot(h, win_ref[:, :3 * WIDTH_A])
    lat = _dot(h, win_ref[:, 3 * WIDTH_A:])
    c_q = lat[:, :Q_LORA]
    c_kv = lat[:, Q_LORA:Q_LORA + KV_LORA]
    kr = lat[:, Q_LORA + KV_LORA:]
    cosp = cos_ref[...]
    sinp = sin_ref[...]
    lane = lax.broadcasted_iota(jnp.int32, cosp.shape, 1)
    cq_tab = (cosp + jnp.where(lane < QK_NOPE, 1.0, 0.0)) * q_scale
    sq_tab = sinp * q_scale

    cqn = _rms(c_q, gq_ref[...]).astype(BF16)
    qf = _dot(cqn, wq_ref[...])
    for hh in range(N_HEADS):
        qh = qf[:, hh * LANES:(hh + 1) * LANES]
        q_ref[0, hh] = (qh * cq_tab + pltpu.roll(qh, ROT_SHIFT, 1) * sq_tab).astype(BF16)

    ckvn = _rms(c_kv, gkv_ref[...]).astype(BF16)
    kf = _dot(ckvn, wk_ref[...])
    krope = kr * cosp + pltpu.roll(kr, ROT_SHIFT, 1) * sinp + jnp.where(lane == SHIFT_LANE, 1.0, 0.0)
    for hh in range(N_HEADS):
        sl = slice(hh * LANES, (hh + 1) * LANES)
        k_ref[0, hh] = (kf[:, sl] + krope).astype(BF16)
    vt_ref[0, 0] = _dot_nt(wvt_ref[...], ckvn).astype(BF16)


def _inproj(x, g, win, cosp, sinp, gq, wq, gkv, wk, wvt, *, tm, q_scale):
    b, s, d = x.shape
    nt = s // tm
    full = lambda a: pl.BlockSpec(a.shape, lambda bi, i: (0,) * a.ndim)
    return pl.pallas_call(
        functools.partial(_inproj_kernel, q_scale=q_scale),
        grid=(b, nt),
        in_specs=[
            pl.BlockSpec((1, tm, d), lambda bi, i: (bi, i, 0)),
            full(g), full(win),
            pl.BlockSpec((tm, LANES), lambda bi, i: (i, 0)),
            pl.BlockSpec((tm, LANES), lambda bi, i: (i, 0)),
            full(gq), full(wq), full(gkv), full(wk), full(wvt),
        ],
        out_specs=[
            pl.BlockSpec((1, tm, 3 * WIDTH_A), lambda bi, i: (bi, i, 0)),
            pl.BlockSpec((1, N_HEADS, tm, LANES), lambda bi, i: (bi, 0, i, 0)),
            pl.BlockSpec((1, N_HEADS, tm, LANES), lambda bi, i: (bi, 0, i, 0)),
            pl.BlockSpec((1, 1, WIDTH_B, tm), lambda bi, i: (bi, i, 0, 0)),
        ],
        out_shape=[
            jax.ShapeDtypeStruct((b, s, 3 * WIDTH_A), F32),
            jax.ShapeDtypeStruct((b, N_HEADS, s, LANES), BF16),
            jax.ShapeDtypeStruct((b, N_HEADS, s, LANES), BF16),
            jax.ShapeDtypeStruct((b, nt, WIDTH_B, tm), BF16),
        ],
        compiler_params=pltpu.CompilerParams(
            dimension_semantics=("parallel", "parallel"), vmem_limit_bytes=VMEM_LIMIT),
        name="inproj",
    )(x, g, win, cosp, sinp, gq, wq, gkv, wk, wvt)


RES_O, RES_L, RES_M = 0, 1, 2
R_FAR = DILATIONS[-1]
R_MID = 4


def _dilated_kernel(q_ref, k_ref, v_ref, tbl_ref, o_ref, kpad, vpad, tblv, tmp, kfar, vfar, res, *, seq, q_scale):
    sb = pl.program_id(2)
    t0 = sb * SUPER
    lp_far = seq // R_FAR + 2 * HALF

    @pl.when(sb == 0)
    def _():
        zeros = jnp.zeros((KV_PAD, LANES), F32)
        kpad[pl.ds(0, KV_PAD), :] = zeros
        vpad[pl.ds(0, KV_PAD), :] = zeros
        kpad[pl.ds(KV_PAD + seq, KV_PAD), :] = zeros
        vpad[pl.ds(KV_PAD + seq, KV_PAD), :] = zeros
        kpad[pl.ds(KV_PAD, seq), :] = k_ref[0]
        vpad[pl.ds(KV_PAD, seq), :] = v_ref[0]
        colk = lax.broadcasted_iota(jnp.int32, (2 * Q_BLOCK, SLAB), 1)
        for gi in range(len(DILATIONS)):
            base = tbl_ref[0, gi]
            tblv[gi, 0] = base
            tblv[gi, 1] = base + jnp.where(colk < HALF, NEG, 0.0)
            tblv[gi, 2] = base + jnp.where(colk >= SLAB - HALF, NEG, 0.0)
        sub = R_FAR // R_MID
        for src, dst in ((kpad, kfar), (vpad, vfar)):
            for c_lo in range(R_MID):
                tmp[...] = src[pl.ds(c_lo, sub * lp_far, stride=R_MID), :]
                for c_hi in range(sub):
                    dst[pl.ds((c_lo + R_MID * c_hi) * lp_far, lp_far), :] = (
                        tmp[pl.ds(c_hi, lp_far, stride=sub), :].astype(BF16))

    lane_k = lax.broadcasted_iota(jnp.int32, (SLAB, LANES), 1)
    lane_q = lax.broadcasted_iota(jnp.int32, (Q_BLOCK, LANES), 1)

    for gi, r in enumerate(DILATIONS):
        nblk_c = SUPER // (Q_BLOCK * r)
        shift = nblk_c.bit_length() - 1

        def body(i, carry, gi=gi, r=r, nblk_c=nblk_c, shift=shift):
            c = lax.shift_right_logical(i, shift)
            n = i & (nblk_c - 1)
            qstart = c + n * (Q_BLOCK * r)
            kstart = t0 + qstart - HALF * r
            qb = q_ref[0, pl.ds(qstart, Q_BLOCK, stride=r), :] * q_scale
            if r == R_FAR:
                row0 = pl.multiple_of(c * lp_far + (sb * nblk_c + n) * Q_BLOCK, Q_BLOCK)
                ks = kfar[pl.ds(row0, SLAB), :]
                vs = vfar[pl.ds(row0, SLAB), :]
            else:
                ks = kpad[pl.ds(KV_PAD + kstart, SLAB, stride=r), :].astype(BF16)
                vs = vpad[pl.ds(KV_PAD + kstart, SLAB, stride=r), :]
            q2 = jnp.concatenate([jnp.where(lane_q < HEAD_DIM_A, qb, 0.0),
                                  jnp.where(lane_q < HEAD_DIM_A, 0.0, qb)], axis=0).astype(BF16)
            variant = jnp.where(kstart < 0, 1, jnp.where(kstart + (SLAB - 1) * r >= seq, 2, 0))
            s = _dot_nt(q2, ks) + tblv[gi, variant]
            m = jnp.max(s, axis=-1, keepdims=True)
            p = jnp.exp2(s - m).astype(BF16)
            one = jnp.ones((), vs.dtype)
            rhs = jnp.concatenate([jnp.where(lane_k < HEAD_DIM_A, vs, one),
                                   jnp.where(lane_k < HEAD_DIM_A, one, vs)], axis=1).astype(BF16)
            o = _dot(p, rhs)
            rows = pl.ds(qstart, Q_BLOCK, stride=r)
            head0 = lane_q < HEAD_DIM_A
            res[gi, RES_O, rows, :] = jnp.where(head0, o[:Q_BLOCK, :LANES], o[Q_BLOCK:, LANES:])
            res[gi, RES_L, rows, :] = jnp.where(head0, o[:Q_BLOCK, LANES:], o[Q_BLOCK:, :LANES])
            res[gi, RES_M, rows, :] = jnp.where(head0, m[:Q_BLOCK], m[Q_BLOCK:])
            return carry

        lax.fori_loop(0, SUPER // Q_BLOCK, body, 0, unroll=8)

    rows = 256

    def combine(i, carry):
        sl = pl.ds(i * rows, rows)
        ms = [res[gi, RES_M, sl, :] for gi in range(len(DILATIONS))]
        mx = functools.reduce(jnp.maximum, ms)
        ws = [jnp.exp2(mm - mx) for mm in ms]
        num = sum(w * res[gi, RES_O, sl, :] for gi, w in enumerate(ws))
        den = sum(w * res[gi, RES_L, sl, :] for gi, w in enumerate(ws))
        o_ref[0, sl, :] = num / den
        return carry

    lax.fori_loop(0, SUPER // rows, combine, 0)


def _dilated_bias_table():
    off = np.abs(np.arange(SLAB)[None, :] - HALF - np.arange(Q_BLOCK)[:, None]).astype(np.float64)
    slopes = 2.0 ** (-8.0 * np.arange(1, N_HEADS + 1) / N_HEADS)
    tbl = np.empty((N_HEADS // 2, len(DILATIONS), 2, Q_BLOCK, SLAB), np.float32)
    for hp in range(N_HEADS // 2):
        for gi, r in enumerate(DILATIONS):
            for hh in range(2):
                bias = -slopes[hp * 2 + hh] * off * r * LOG2E
                tbl[hp, gi, hh] = np.where(off <= HALF, bias, NEG)
    return tbl.reshape(N_HEADS // 2, len(DILATIONS), 2 * Q_BLOCK, SLAB)


def _dilated_attention(pa):
    b, s, _ = pa.shape
    assert s % SUPER == 0 and s // DILATIONS[-1] >= 2 * Q_BLOCK
    nsb = s // SUPER
    npair = N_HEADS // 2
    tbl = jnp.asarray(_dilated_bias_table())
    q_scale = HEAD_DIM_A ** -0.5 * LOG2E
    return pl.pallas_call(
        functools.partial(_dilated_kernel, seq=s, q_scale=q_scale),
        grid=(b, npair, nsb),
        in_specs=[
            pl.BlockSpec((1, SUPER, LANES), lambda bi, hp, sb: (bi, sb, hp)),
            pl.BlockSpec((1, s, LANES), lambda bi, hp, sb: (bi, 0, npair + hp)),
            pl.BlockSpec((1, s, LANES), lambda bi, hp, sb: (bi, 0, 2 * npair + hp)),
            pl.BlockSpec((1, len(DILATIONS), 2 * Q_BLOCK, SLAB), lambda bi, hp, sb: (hp, 0, 0, 0)),
        ],
        out_specs=pl.BlockSpec((1, SUPER, LANES), lambda bi, hp, sb: (bi, sb, hp)),
        out_shape=jax.ShapeDtypeStruct((b, s, WIDTH_A), F32),
        scratch_shapes=[
            pltpu.VMEM((s + 2 * KV_PAD, LANES), F32),
            pltpu.VMEM((s + 2 * KV_PAD, LANES), F32),
            pltpu.VMEM((len(DILATIONS), 3, 2 * Q_BLOCK, SLAB), F32),
            pltpu.VMEM((s // R_MID + 2 * HALF * (R_FAR // R_MID), LANES), F32),
            pltpu.VMEM((s + 2 * HALF * R_FAR, LANES), BF16),
            pltpu.VMEM((s + 2 * HALF * R_FAR, LANES), BF16),
            pltpu.VMEM((len(DILATIONS), 3, SUPER, LANES), F32),
        ],
        compiler_params=pltpu.CompilerParams(
            dimension_semantics=("parallel", "parallel", "arbitrary"), vmem_limit_bytes=VMEM_LIMIT),
        name="dilated_attn",
    )(pa, pa, pa, tbl)


def _mla_kernel(q_ref, k_ref, vt_ref, o_ref, s_a, s_b, p_a, p_b, *, tk):
    q = q_ref[0, 0]
    tq = q.shape[0]
    nk = k_ref.shape[2] // tk

    def scores(j, s_buf):
        ks = k_ref[0, 0, pl.ds(pl.multiple_of(j * tk, tk), tk), :]
        s = _dot_nt(ks, q)
        s_buf[...] = s
        return jnp.max(s, axis=0, keepdims=True)

    def softmax(s_buf, p_buf, tmax, m):
        mn = jnp.maximum(m, tmax)
        p_buf[...] = jnp.exp2(s_buf[...] - mn).astype(BF16)
        return mn, jnp.exp2(m - mn)

    ones_rows = jnp.ones((ONES_ROWS, tk), BF16)

    def pv(j, p_buf, alpha, acc):
        return alpha * acc + _dot(jnp.concatenate([vt_ref[0, j], ones_rows], axis=0), p_buf[...])

    def trip(i, carry, last=False):
        t_a, al_b, m, acc = carry
        t_b = scores(2 * i + 1, s_b)
        m, al_a = softmax(s_a, p_a, t_a, m)
        acc = pv(jnp.maximum(2 * i - 1, 0), p_b, al_b, acc)
        if not last:
            t_a = scores(2 * i + 2, s_a)
        m, al_b = softmax(s_b, p_b, t_b, m)
        acc = pv(2 * i, p_a, al_a, acc)
        return t_a, al_b, m, acc

    p_b[...] = jnp.zeros_like(p_b)
    carry = (scores(0, s_a), jnp.ones((1, tq), F32), jnp.full((1, tq), -jnp.inf, F32),
             jnp.zeros((V_DIM + ONES_ROWS, tq), F32))
    carry = lax.fori_loop(0, nk // 2 - 1, trip, carry)
    _, al_b, _, acc = trip(nk // 2 - 1, carry, last=True)
    acc = pv(nk - 1, p_b, al_b, acc)
    o_ref[0] = acc[:V_DIM] / acc[V_DIM:V_DIM + 1]


def _mla_bound_kernel(kmax_ref, q_ref, k_ref, vt_ref, o_ref, qa, p_a, p_b, *, tk):
    bh = pl.program_id(0) * pl.num_programs(1) + pl.program_id(1)
    qf = q_ref[0, 0].astype(F32)
    tq = qf.shape[0]
    nk = k_ref.shape[2] // tk
    shift = jnp.sqrt(jnp.sum(qf * qf, axis=-1, keepdims=True)) * (kmax_ref[bh] * BOUND_SLACK)
    lane = lax.broadcasted_iota(jnp.int32, qf.shape, 1)
    qa[...] = jnp.where(lane == SHIFT_LANE, -shift, qf).astype(BF16)
    ones_rows = jnp.ones((ONES_ROWS, tk), BF16)

    def probs(j, p_buf):
        ks = k_ref[0, 0, pl.ds(pl.multiple_of(j * tk, tk), tk), :]
        p_buf[...] = jnp.exp2(_dot_nt(ks, qa[...])).astype(BF16)

    def pv(j, p_buf, acc):
        return acc + _dot(jnp.concatenate([vt_ref[0, j], ones_rows], axis=0), p_buf[...])

    def trip(i, acc):
        probs(2 * i, p_a)
        acc = pv(jnp.maximum(2 * i - 1, 0), p_b, acc)
        probs(2 * i + 1, p_b)
        return pv(2 * i, p_a, acc)

    p_b[...] = jnp.zeros_like(p_b)
    acc = lax.fori_loop(0, nk // 2, trip, jnp.zeros((V_DIM + ONES_ROWS, tq), F32))
    acc = pv(nk - 1, p_b, acc)
    o_ref[0] = acc[:V_DIM] / acc[V_DIM:V_DIM + 1]


def _mla_attention(q, k, vt, *, tq):
    b, nh, s, _ = q.shape
    _, nkt, _, tk = vt.shape
    specs = dict(
        grid=(b, nh, s // tq),
        out_specs=pl.BlockSpec((1, V_DIM, tq), lambda bi, h, i: (bi, h, i)),
        out_shape=jax.ShapeDtypeStruct((b, nh * V_DIM, s), F32),
        compiler_params=pltpu.CompilerParams(
            dimension_semantics=("parallel", "parallel", "arbitrary"), vmem_limit_bytes=VMEM_LIMIT),
    )
    qkv_specs = [
        pl.BlockSpec((1, 1, tq, LANES), lambda bi, h, i: (bi, h, i, 0)),
        pl.BlockSpec((1, 1, s, LANES), lambda bi, h, i: (bi, h, 0, 0)),
        pl.BlockSpec((1, nkt, V_DIM, tk), lambda bi, h, i: (bi, 0, h, 0)),
    ]

    def online(q, k, vt, kmax):
        return pl.pallas_call(
            functools.partial(_mla_kernel, tk=tk), in_specs=qkv_specs,
            scratch_shapes=[pltpu.VMEM((tk, tq), F32), pltpu.VMEM((tk, tq), F32),
                            pltpu.VMEM((tk, tq), BF16), pltpu.VMEM((tk, tq), BF16)],
            name="mla_attn", **specs)(q, k, vt)

    def bounded(q, k, vt, kmax):
        return pl.pallas_call(
            functools.partial(_mla_bound_kernel, tk=tk),
            in_specs=[pl.BlockSpec(memory_space=pltpu.SMEM)] + qkv_specs,
            scratch_shapes=[pltpu.VMEM((tq, LANES), BF16), pltpu.VMEM((tk, tq), BF16),
                            pltpu.VMEM((tk, tq), BF16)],
            name="mla_attn_bound", **specs)(kmax.reshape(-1), q, k, vt)

    norm_max = lambda t: jnp.sqrt(jnp.max(jnp.sum(jnp.square(t.astype(F32)), axis=-1), axis=-1))
    kmax = norm_max(k)
    small = jnp.max(norm_max(q) * kmax) * BOUND_SLACK <= MAX_SHIFT
    return lax.cond(small, bounded, online, q, k, vt, kmax)


def _outproj_kernel(ya_ref, ybt_ref, x_ref, ga_ref, gb_ref, wo_ref, gp_ref, o_ref):
    yan = _rms(ya_ref[0], ga_ref[...]).astype(BF16)
    ybn = _rms(ybt_ref[0].T, gb_ref[...]).astype(BF16)
    y = _dot(yan, wo_ref[:WIDTH_A, :]) + _dot(ybn, wo_ref[WIDTH_A:, :])
    o_ref[0] = x_ref[0] + _rms(y, gp_ref[...])


def _outproj(ya, ybt, x, ga, gb, wo, gp, *, tm):
    b, s, d = x.shape
    full = lambda a: pl.BlockSpec(a.shape, lambda bi, i: (0,) * a.ndim)
    return pl.pallas_call(
        _outproj_kernel,
        grid=(b, s // tm),
        in_specs=[
            pl.BlockSpec((1, tm, WIDTH_A), lambda bi, i: (bi, i, 0)),
            pl.BlockSpec((1, WIDTH_B, tm), lambda bi, i: (bi, 0, i)),
            pl.BlockSpec((1, tm, d), lambda bi, i: (bi, i, 0)),
            full(ga), full(gb), full(wo), full(gp),
        ],
        out_specs=pl.BlockSpec((1, tm, d), lambda bi, i: (bi, i, 0)),
        out_shape=jax.ShapeDtypeStruct((b, s, d), F32),
        compiler_params=pltpu.CompilerParams(
            dimension_semantics=("parallel", "parallel"), vmem_limit_bytes=VMEM_LIMIT),
        name="outproj",
    )(ya, ybt, x, ga, gb, wo, gp)


FF_GROUPS = ((0, 4), (4, 8), (8, 11))


def _ffn_kernel(taps_ref, xp_ref, x_ref, xn_ref, g_ref, wup_ref, cw_ref, cb_ref, wd_ref, gp_ref, o_ref,
                ubuf, abuf, *, tm):
    i = pl.program_id(1)
    g = g_ref[...]
    hp = _rms(xp_ref[0], g) * jnp.where(i > 0, 1.0, 0.0)
    hn = _rms(xn_ref[0], g) * jnp.where(i < pl.num_programs(1) - 1, 1.0, 0.0)
    h = jnp.concatenate([hp, _rms(x_ref[0], g), hn], axis=0).astype(BF16)
    nslab = FF_CHUNK // LANES
    nc = D_FF // FF_CHUNK

    def up(j, slot):
        for part in range(2):
            c0 = part * D_FF + j * FF_CHUNK
            u = _dot(h, wup_ref[:, c0:c0 + FF_CHUNK])
            for t in range(nslab):
                ubuf[slot, part, t] = u[:, t * LANES:(t + 1) * LANES]

    def conv(slot, part, j, t):
        c0 = part * D_FF + j * FF_CHUNK + t * LANES
        cw = cw_ref[:, c0:c0 + LANES]
        acc = cb_ref[:, c0:c0 + LANES]
        for tap in range(3):
            acc = acc + ubuf[slot, part, t, pl.ds(taps_ref[tap], tm, stride=1), :] * cw[tap:tap + 1]
        return acc

    def gate(j, slot):
        for t in range(nslab):
            a = jax.nn.gelu(conv(slot, 0, j, t), approximate=True) * conv(slot, 1, j, t)
            abuf[:, j * FF_CHUNK + t * LANES:j * FF_CHUNK + (t + 1) * LANES] = a.astype(BF16)

    up(0, 0)
    y = None
    for j in range(nc):
        if j + 1 < nc:
            up(j + 1, (j + 1) % 2)
        gate(j, j % 2)
        for lo, hi in FF_GROUPS:
            if j + 1 == hi:
                part = _dot(abuf[:, lo * FF_CHUNK:hi * FF_CHUNK], wd_ref[lo * FF_CHUNK:hi * FF_CHUNK, :])
                y = part if y is None else y + part
    o_ref[0] = x_ref[0] + _rms(y, gp_ref[...])


def _ffn(x, g, wup, cw, cb, wd, gp, *, tm):
    b, s, d = x.shape
    nt = s // tm
    hb = tm // HALO
    last = s // HALO - 1
    taps = jnp.arange(HALO - 1, HALO + 2, dtype=jnp.int32)
    resident = lambda a: pl.BlockSpec(a.shape, lambda bi, i: (0,) * a.ndim, pipeline_mode=pl.Buffered(1))
    return pl.pallas_call(
        functools.partial(_ffn_kernel, tm=tm),
        grid=(b, nt),
        in_specs=[
            pl.BlockSpec(memory_space=pltpu.SMEM),
            pl.BlockSpec((1, HALO, d), lambda bi, i: (bi, jnp.maximum(i * hb - 1, 0), 0)),
            pl.BlockSpec((1, tm, d), lambda bi, i: (bi, i, 0)),
            pl.BlockSpec((1, HALO, d), lambda bi, i: (bi, jnp.minimum((i + 1) * hb, last), 0)),
            resident(g), resident(wup), resident(cw), resident(cb), resident(wd), resident(gp),
        ],
        out_specs=pl.BlockSpec((1, tm, d), lambda bi, i: (bi, i, 0)),
        out_shape=jax.ShapeDtypeStruct((b, s, d), F32),
        scratch_shapes=[
            pltpu.VMEM((2, 2, FF_CHUNK // LANES, tm + 2 * HALO, LANES), F32),
            pltpu.VMEM((tm, D_FF), BF16),
        ],
        compiler_params=pltpu.CompilerParams(
            dimension_semantics=("parallel", "parallel"), vmem_limit_bytes=VMEM_LIMIT),
        name="convffn",
    )(taps, x, x, x, g, wup, cw, cb, wd, gp)


def _rot_cols(w):
    half = w.shape[-1] // 2
    return jnp.concatenate([-w[..., half:], w[..., :half]], axis=-1)


def _pad_cols(w, before, total):
    return jnp.pad(w, [(0, 0)] * (w.ndim - 1) + [(before, total - before - w.shape[-1])])


def _layer(x, norm_mix_pre, w_in, q_lat_norm, w_uq, kv_lat_norm, w_ukv, out_norm_a, out_norm_b, w_o,
           norm_mix_post, norm_ffn_pre, w_up, conv_w, conv_b, w_down, norm_ffn_post):
    b, s, d = x.shape
    row = lambda v: v.reshape(1, -1).astype(F32)

    c0 = 3 * WIDTH_A + Q_LORA + KV_LORA
    w_kr = w_in[:, c0:]
    win = jnp.concatenate(
        [w_in[:, :c0], _pad_cols(jnp.concatenate([w_kr, _rot_cols(w_kr)], axis=1), QK_NOPE, LANES)],
        axis=1).astype(BF16)
    wq3 = w_uq.reshape(Q_LORA, N_HEADS, QK_NOPE + QK_ROPE)
    wq = jnp.concatenate([wq3, _rot_cols(wq3[..., QK_NOPE:])], axis=-1).reshape(
        Q_LORA, N_HEADS * LANES).astype(BF16)
    wkv3 = w_ukv.reshape(KV_LORA, N_HEADS, QK_NOPE + V_DIM)
    wk = _pad_cols(wkv3[..., :QK_NOPE], 0, LANES).reshape(KV_LORA, N_HEADS * LANES).astype(BF16)
    wvt = wkv3[..., QK_NOPE:].reshape(KV_LORA, WIDTH_B).T.astype(BF16)

    pos = jnp.arange(s, dtype=F32)
    inv_freq = jnp.exp(-math.log(ROPE_BASE) * jnp.arange(0, QK_ROPE, 2, dtype=F32) / QK_ROPE)
    ang = pos[:, None] * inv_freq[None, :]
    cosp = _pad_cols(jnp.tile(jnp.cos(ang), (1, 2)), QK_NOPE, LANES)
    sinp = _pad_cols(jnp.tile(jnp.sin(ang), (1, 2)), QK_NOPE, LANES)

    q_scale = (QK_NOPE + QK_ROPE) ** -0.5 * LOG2E
    pa, q, k, vt = _inproj(x, row(norm_mix_pre), win, cosp, sinp, row(q_lat_norm), wq,
                           row(kv_lat_norm), wk, wvt, tm=512, q_scale=q_scale)
    ya = _dilated_attention(pa)
    ybt = _mla_attention(q, k, vt, tq=4096)
    x1 = _outproj(ya, ybt, x, row(out_norm_a), row(out_norm_b), w_o.astype(BF16), row(norm_mix_post), tm=512)
    return _ffn(x1, row(norm_ffn_pre), w_up.astype(BF16), conv_w, conv_b.reshape(1, -1),
                w_down.astype(BF16), row(norm_ffn_post), tm=512)


def kernel(x, norm_mix_pre, w_in, q_lat_norm, w_uq, kv_lat_norm, w_ukv, out_norm_a, out_norm_b, w_o,
           norm_mix_post, norm_ffn_pre, w_up, conv_w, conv_b, w_down, norm_ffn_post):
    for l in range(norm_mix_pre.shape[0]):
        x = _layer(x, norm_mix_pre[l], w_in[l], q_lat_norm[l], w_uq[l], kv_lat_norm[l], w_ukv[l],
                   out_norm_a[l], out_norm_b[l], w_o[l], norm_mix_post[l], norm_ffn_pre[l], w_up[l],
                   conv_w[l], conv_b[l], w_down[l], norm_ffn_post[l])
    return x
```

```python
import functools
import math

import numpy as np
import jax
import jax.numpy as jnp
from jax import lax
from jax.experimental import pallas as pl
from jax.experimental.pallas import tpu as pltpu

F32 = jnp.float32
BF16 = jnp.bfloat16

EPS = 1e-6
NEG = -1e30
LOG2E = 1.4426950408889634

D_MODEL = 1024
N_HEADS = 8
HEAD_DIM_A = 64
WIDTH_A = N_HEADS * HEAD_DIM_A
DILATIONS = (1, 4, 16)
HALF = 64
Q_BLOCK = 128
SLAB = Q_BLOCK + 2 * HALF
SUPER = Q_BLOCK * DILATIONS[-1]
KV_PAD = HALF * DILATIONS[-1]
QK_NOPE = 64
QK_ROPE = 32
V_DIM = 64
WIDTH_B = N_HEADS * V_DIM
Q_LORA = 384
KV_LORA = 256
ROPE_BASE = 10000.0
D_FF = 2816
FF_CHUNK = 256
LANES = 128
HALO = 16
ONES_ROWS = 16

VMEM_LIMIT = 56 * 1024 * 1024


def _rms(x, g):
    return x * lax.rsqrt(jnp.mean(x * x, axis=-1, keepdims=True) + EPS) * g


def _dot(a, b):
    return jnp.dot(a, b, preferred_element_type=F32)


def _dot_nt(a, b):
    return lax.dot_general(a, b, (((1,), (1,)), ((), ())), preferred_element_type=F32)


ROT_SHIFT = LANES - QK_ROPE
SHIFT_LANE = QK_NOPE + QK_ROPE
BOUND_SLACK = 1.02
MAX_SHIFT = 60.0


def _inproj_kernel(x_ref, g_ref, win_ref, cos_ref, sin_ref, gq_ref, wq_ref, gkv_ref, wk_ref,
                   wvt_ref, ind_ref, pa_ref, q_ref, k_ref, vt_ref, st_ref, *, q_scale):
    h = _rms(x_ref[0], g_ref[...]).astype(BF16)
    pa_ref[0] = _dot(h, win_ref[:, :3 * WIDTH_A])
    lat = _dot(h, win_ref[:, 3 * WIDTH_A:])
    c_q = lat[:, :Q_LORA]
    c_kv = lat[:, Q_LORA:Q_LORA + KV_LORA]
    kr = lat[:, Q_LORA + KV_LORA:]
    cosp = cos_ref[...]
    sinp = sin_ref[...]
    lane = lax.broadcasted_iota(jnp.int32, cosp.shape, 1)
    cq_tab = (cosp + jnp.where(lane < QK_NOPE, 1.0, 0.0)) * q_scale
    sq_tab = sinp * q_scale

    def max_sq_norm(rows):
        sq = jnp.concatenate([(v * v).astype(BF16) for v in rows], axis=1)
        return jnp.max(_dot(sq, ind_ref[...]), axis=0, keepdims=True)

    cqn = _rms(c_q, gq_ref[...]).astype(BF16)
    qf = _dot(cqn, wq_ref[...])
    qs = []
    for hh in range(N_HEADS):
        qh = qf[:, hh * LANES:(hh + 1) * LANES]
        qs.append(qh * cq_tab + pltpu.roll(qh, ROT_SHIFT, 1) * sq_tab)
        q_ref[0, hh] = qs[-1].astype(BF16)

    ckvn = _rms(c_kv, gkv_ref[...]).astype(BF16)
    kf = _dot(ckvn, wk_ref[...])
    krope = kr * cosp + pltpu.roll(kr, ROT_SHIFT, 1) * sinp + jnp.where(lane == SHIFT_LANE, 1.0, 0.0)
    ks = []
    for hh in range(N_HEADS):
        ks.append(kf[:, hh * LANES:(hh + 1) * LANES] + krope)
        k_ref[0, hh] = ks[-1].astype(BF16)
    st_ref[0, 0] = jnp.concatenate(
        [max_sq_norm(qs), max_sq_norm(ks), jnp.zeros((6, LANES), F32)], axis=0)
    vt_ref[0, 0] = _dot_nt(wvt_ref[...], ckvn).astype(BF16)


def _inproj(x, g, win, cosp, sinp, gq, wq, gkv, wk, wvt, *, tm, q_scale):
    b, s, d = x.shape
    nt = s // tm
    ind = jnp.asarray(np.repeat(np.eye(N_HEADS, LANES, dtype=np.float32), LANES, axis=0), BF16)
    full = lambda a: pl.BlockSpec(a.shape, lambda bi, i: (0,) * a.ndim)
    return pl.pallas_call(
        functools.partial(_inproj_kernel, q_scale=q_scale),
        grid=(b, nt),
        in_specs=[
            pl.BlockSpec((1, tm, d), lambda bi, i: (bi, i, 0)),
            full(g), full(win),
            pl.BlockSpec((tm, LANES), lambda bi, i: (i, 0)),
            pl.BlockSpec((tm, LANES), lambda bi, i: (i, 0)),
            full(gq), full(wq), full(gkv), full(wk), full(wvt), full(ind),
        ],
        out_specs=[
            pl.BlockSpec((1, tm, 3 * WIDTH_A), lambda bi, i: (bi, i, 0)),
            pl.BlockSpec((1, N_HEADS, tm, LANES), lambda bi, i: (bi, 0, i, 0)),
            pl.BlockSpec((1, N_HEADS, tm, LANES), lambda bi, i: (bi, 0, i, 0)),
            pl.BlockSpec((1, 1, WIDTH_B, tm), lambda bi, i: (bi, i, 0, 0)),
            pl.BlockSpec((1, 1, 8, LANES), lambda bi, i: (bi, i, 0, 0)),
        ],
        out_shape=[
            jax.ShapeDtypeStruct((b, s, 3 * WIDTH_A), F32),
            jax.ShapeDtypeStruct((b, N_HEADS, s, LANES), BF16),
            jax.ShapeDtypeStruct((b, N_HEADS, s, LANES), BF16),
            jax.ShapeDtypeStruct((b, nt, WIDTH_B, tm), BF16),
            jax.ShapeDtypeStruct((b, nt, 8, LANES), F32),
        ],
        compiler_params=pltpu.CompilerParams(
            dimension_semantics=("parallel", "parallel"), vmem_limit_bytes=VMEM_LIMIT),
        name="inproj",
    )(x, g, win, cosp, sinp, gq, wq, gkv, wk, wvt, ind)


RES_O, RES_L, RES_M = 0, 1, 2
R_FAR = DILATIONS[-1]
R_MID = 4


def _dilated_kernel(q_ref, k_ref, v_ref, tbl_ref, o_ref, kpad, vpad, tblv, tmp, kfar, vfar, res, *, seq, q_scale):
    sb = pl.program_id(2)
    t0 = sb * SUPER
    lp_far = seq // R_FAR + 2 * HALF

    @pl.when(sb == 0)
    def _():
        zeros = jnp.zeros((KV_PAD, LANES), F32)
        kpad[pl.ds(0, KV_PAD), :] = zeros
        vpad[pl.ds(0, KV_PAD), :] = zeros
        kpad[pl.ds(KV_PAD + seq, KV_PAD), :] = zeros
        vpad[pl.ds(KV_PAD + seq, KV_PAD), :] = zeros
        kpad[pl.ds(KV_PAD, seq), :] = k_ref[0]
        vpad[pl.ds(KV_PAD, seq), :] = v_ref[0]
        colk = lax.broadcasted_iota(jnp.int32, (2 * Q_BLOCK, SLAB), 1)
        for gi in range(len(DILATIONS)):
            base = tbl_ref[0, gi]
            tblv[gi, 0] = base
            tblv[gi, 1] = base + jnp.where(colk < HALF, NEG, 0.0)
            tblv[gi, 2] = base + jnp.where(colk >= SLAB - HALF, NEG, 0.0)
        sub = R_FAR // R_MID
        for src, dst in ((kpad, kfar), (vpad, vfar)):
            for c_lo in range(R_MID):
                tmp[...] = src[pl.ds(c_lo, sub * lp_far, stride=R_MID), :]
                for c_hi in range(sub):
                    dst[pl.ds((c_lo + R_MID * c_hi) * lp_far, lp_far), :] = (
                        tmp[pl.ds(c_hi, lp_far, stride=sub), :].astype(BF16))

    lane_k = lax.broadcasted_iota(jnp.int32, (SLAB, LANES), 1)
    lane_q = lax.broadcasted_iota(jnp.int32, (Q_BLOCK, LANES), 1)

    for gi, r in enumerate(DILATIONS):
        nblk_c = SUPER // (Q_BLOCK * r)
        shift = nblk_c.bit_length() - 1

        def body(i, carry, gi=gi, r=r, nblk_c=nblk_c, shift=shift):
            c = lax.shift_right_logical(i, shift)
            n = i & (nblk_c - 1)
            qstart = c + n * (Q_BLOCK * r)
            kstart = t0 + qstart - HALF * r
            qb = q_ref[0, pl.ds(qstart, Q_BLOCK, stride=r), :] * q_scale
            if r == R_FAR:
                row0 = pl.multiple_of(c * lp_far + (sb * nblk_c + n) * Q_BLOCK, Q_BLOCK)
                ks = kfar[pl.ds(row0, SLAB), :]
                vs = vfar[pl.ds(row0, SLAB), :]
            else:
                ks = kpad[pl.ds(KV_PAD + kstart, SLAB, stride=r), :].astype(BF16)
                vs = vpad[pl.ds(KV_PAD + kstart, SLAB, stride=r), :]
            q2 = jnp.concatenate([jnp.where(lane_q < HEAD_DIM_A, qb, 0.0),
                                  jnp.where(lane_q < HEAD_DIM_A, 0.0, qb)], axis=0).astype(BF16)
            variant = jnp.where(kstart < 0, 1, jnp.where(kstart + (SLAB - 1) * r >= seq, 2, 0))
            s = _dot_nt(q2, ks) + tblv[gi, variant]
            m = jnp.max(s, axis=-1, keepdims=True)
            p = jnp.exp2(s - m).astype(BF16)
            one = jnp.ones((), vs.dtype)
            rhs = jnp.concatenate([jnp.where(lane_k < HEAD_DIM_A, vs, one),
                                   jnp.where(lane_k < HEAD_DIM_A, one, vs)], axis=1).astype(BF16)
            o = _dot(p, rhs)
            rows = pl.ds(qstart, Q_BLOCK, stride=r)
            head0 = lane_q < HEAD_DIM_A
            res[gi, RES_O, rows, :] = jnp.where(head0, o[:Q_BLOCK, :LANES], o[Q_BLOCK:, LANES:])
            res[gi, RES_L, rows, :] = jnp.where(head0, o[:Q_BLOCK, LANES:], o[Q_BLOCK:, :LANES])
            res[gi, RES_M, rows, :] = jnp.where(head0, m[:Q_BLOCK], m[Q_BLOCK:])
            return carry

        lax.fori_loop(0, SUPER // Q_BLOCK, body, 0, unroll=8)

    rows = 256

    def combine(i, carry):
        sl = pl.ds(i * rows, rows)
        ms = [res[gi, RES_M, sl, :] for gi in range(len(DILATIONS))]
        mx = functools.reduce(jnp.maximum, ms)
        ws = [jnp.exp2(mm - mx) for mm in ms]
        num = sum(w * res[gi, RES_O, sl, :] for gi, w in enumerate(ws))
        den = sum(w * res[gi, RES_L, sl, :] for gi, w in enumerate(ws))
        o_ref[0, sl, :] = num / den
        return carry

    lax.fori_loop(0, SUPER // rows, combine, 0)


def _dilated_bias_table():
    off = np.abs(np.arange(SLAB)[None, :] - HALF - np.arange(Q_BLOCK)[:, None]).astype(np.float64)
    slopes = 2.0 ** (-8.0 * np.arange(1, N_HEADS + 1) / N_HEADS)
    tbl = np.empty((N_HEADS // 2, len(DILATIONS), 2, Q_BLOCK, SLAB), np.float32)
    for hp in range(N_HEADS // 2):
        for gi, r in enumerate(DILATIONS):
            for hh in range(2):
                bias = -slopes[hp * 2 + hh] * off * r * LOG2E
                tbl[hp, gi, hh] = np.where(off <= HALF, bias, NEG)
    return tbl.reshape(N_HEADS // 2, len(DILATIONS), 2 * Q_BLOCK, SLAB)


def _dilated_attention(pa):
    b, s, _ = pa.shape
    assert s % SUPER == 0 and s // DILATIONS[-1] >= 2 * Q_BLOCK
    nsb = s // SUPER
    npair = N_HEADS // 2
    tbl = jnp.asarray(_dilated_bias_table())
    q_scale = HEAD_DIM_A ** -0.5 * LOG2E
    return pl.pallas_call(
        functools.partial(_dilated_kernel, seq=s, q_scale=q_scale),
        grid=(b, npair, nsb),
        in_specs=[
            pl.BlockSpec((1, SUPER, LANES), lambda bi, hp, sb: (bi, sb, hp)),
            pl.BlockSpec((1, s, LANES), lambda bi, hp, sb: (bi, 0, npair + hp)),
            pl.BlockSpec((1, s, LANES), lambda bi, hp, sb: (bi, 0, 2 * npair + hp)),
            pl.BlockSpec((1, len(DILATIONS), 2 * Q_BLOCK, SLAB), lambda bi, hp, sb: (hp, 0, 0, 0)),
        ],
        out_specs=pl.BlockSpec((1, SUPER, LANES), lambda bi, hp, sb: (bi, sb, hp)),
        out_shape=jax.ShapeDtypeStruct((b, s, WIDTH_A), F32),
        scratch_shapes=[
            pltpu.VMEM((s + 2 * KV_PAD, LANES), F32),
            pltpu.VMEM((s + 2 * KV_PAD, LANES), F32),
            pltpu.VMEM((len(DILATIONS), 3, 2 * Q_BLOCK, SLAB), F32),
            pltpu.VMEM((s // R_MID + 2 * HALF * (R_FAR // R_MID), LANES), F32),
            pltpu.VMEM((s + 2 * HALF * R_FAR, LANES), BF16),
            pltpu.VMEM((s + 2 * HALF * R_FAR, LANES), BF16),
            pltpu.VMEM((len(DILATIONS), 3, SUPER, LANES), F32),
        ],
        compiler_params=pltpu.CompilerParams(
            dimension_semantics=("parallel", "parallel", "arbitrary"), vmem_limit_bytes=VMEM_LIMIT),
        name="dilated_attn",
    )(pa, pa, pa, tbl)


def _mla_kernel(q_ref, k_ref, vt_ref, o_ref, s_a, s_b, p_a, p_b, *, tk):
    q = q_ref[0, 0]
    tq = q.shape[0]
    nk = k_ref.shape[2] // tk

    def scores(j, s_buf):
        ks = k_ref[0, 0, pl.ds(pl.multiple_of(j * tk, tk), tk), :]
        s = _dot_nt(ks, q)
        s_buf[...] = s
        return jnp.max(s, axis=0, keepdims=True)

    def softmax(s_buf, p_buf, tmax, m):
        mn = jnp.maximum(m, tmax)
        p_buf[...] = jnp.exp2(s_buf[...] - mn).astype(BF16)
        return mn, jnp.exp2(m - mn)

    ones_rows = jnp.ones((ONES_ROWS, tk), BF16)

    def pv(j, p_buf, alpha, acc):
        return alpha * acc + _dot(jnp.concatenate([vt_ref[0, j], ones_rows], axis=0), p_buf[...])

    def trip(i, carry, last=False):
        t_a, al_b, m, acc = carry
        t_b = scores(2 * i + 1, s_b)
        m, al_a = softmax(s_a, p_a, t_a, m)
        acc = pv(jnp.maximum(2 * i - 1, 0), p_b, al_b, acc)
        if not last:
            t_a = scores(2 * i + 2, s_a)
        m, al_b = softmax(s_b, p_b, t_b, m)
        acc = pv(2 * i, p_a, al_a, acc)
        return t_a, al_b, m, acc

    p_b[...] = jnp.zeros_like(p_b)
    carry = (scores(0, s_a), jnp.ones((1, tq), F32), jnp.full((1, tq), -jnp.inf, F32),
             jnp.zeros((V_DIM + ONES_ROWS, tq), F32))
    carry = lax.fori_loop(0, nk // 2 - 1, trip, carry)
    _, al_b, _, acc = trip(nk // 2 - 1, carry, last=True)
    acc = pv(nk - 1, p_b, al_b, acc)
    o_ref[0] = acc[:V_DIM] / acc[V_DIM:V_DIM + 1]


def _mla_bound_kernel(shift_ref, q_ref, k_ref, vt_ref, o_ref, qa, p_a, p_b, *, tk):
    bh = pl.program_id(0) * pl.num_programs(1) + pl.program_id(1)
    q = q_ref[0, 0]
    tq = q.shape[0]
    nk = k_ref.shape[2] // tk
    lane = lax.broadcasted_iota(jnp.int32, q.shape, 1)
    qa[...] = jnp.where(lane == SHIFT_LANE, -shift_ref[bh], q.astype(F32)).astype(BF16)
    ones_rows = jnp.ones((ONES_ROWS, tk), BF16)

    def probs(j, p_buf):
        ks = k_ref[0, 0, pl.ds(pl.multiple_of(j * tk, tk), tk), :]
        p_buf[...] = jnp.exp2(_dot_nt(ks, qa[...])).astype(BF16)

    def pv(j, p_buf, acc):
        return acc + _dot(jnp.concatenate([vt_ref[0, j], ones_rows], axis=0), p_buf[...])

    def trip(i, acc):
        probs(2 * i, p_a)
        acc = pv(jnp.maximum(2 * i - 1, 0), p_b, acc)
        probs(2 * i + 1, p_b)
        return pv(2 * i, p_a, acc)

    p_b[...] = jnp.zeros_like(p_b)
    acc = lax.fori_loop(0, nk // 2, trip, jnp.zeros((V_DIM + ONES_ROWS, tq), F32))
    acc = pv(nk - 1, p_b, acc)
    o_ref[0] = acc[:V_DIM] / acc[V_DIM:V_DIM + 1]


def _mla_attention(q, k, vt, stats, *, tq):
    b, nh, s, _ = q.shape
    _, nkt, _, tk = vt.shape
    specs = dict(
        grid=(b, nh, s // tq),
        out_specs=pl.BlockSpec((1, V_DIM, tq), lambda bi, h, i: (bi, h, i)),
        out_shape=jax.ShapeDtypeStruct((b, nh * V_DIM, s), F32),
        compiler_params=pltpu.CompilerParams(
            dimension_semantics=("parallel", "parallel", "arbitrary"), vmem_limit_bytes=VMEM_LIMIT),
    )
    qkv_specs = [
        pl.BlockSpec((1, 1, tq, LANES), lambda bi, h, i: (bi, h, i, 0)),
        pl.BlockSpec((1, 1, s, LANES), lambda bi, h, i: (bi, h, 0, 0)),
        pl.BlockSpec((1, nkt, V_DIM, tk), lambda bi, h, i: (bi, 0, h, 0)),
    ]

    def online(q, k, vt, shift):
        return pl.pallas_call(
            functools.partial(_mla_kernel, tk=tk), in_specs=qkv_specs,
            scratch_shapes=[pltpu.VMEM((tk, tq), F32), pltpu.VMEM((tk, tq), F32),
                            pltpu.VMEM((tk, tq), BF16), pltpu.VMEM((tk, tq), BF16)],
            name="mla_attn", **specs)(q, k, vt)

    def bounded(q, k, vt, shift):
        return pl.pallas_call(
            functools.partial(_mla_bound_kernel, tk=tk),
            in_specs=[pl.BlockSpec(memory_space=pltpu.SMEM)] + qkv_specs,
            scratch_shapes=[pltpu.VMEM((tq, LANES), BF16), pltpu.VMEM((tk, tq), BF16),
                            pltpu.VMEM((tk, tq), BF16)],
            name="mla_attn_bound", **specs)(shift.reshape(-1), q, k, vt)

    norms = jnp.sqrt(jnp.max(stats[:, :, :2, :nh], axis=1))
    shift = norms[:, 0] * norms[:, 1] * BOUND_SLACK
    return lax.cond(jnp.max(shift) <= MAX_SHIFT, bounded, online, q, k, vt, shift)


def _outproj_kernel(ya_ref, ybt_ref, x_ref, ga_ref, gb_ref, wo_ref, gp_ref, o_ref):
    yan = _rms(ya_ref[0], ga_ref[...]).astype(BF16)
    ybn = _rms(ybt_ref[0].T, gb_ref[...]).astype(BF16)
    y = _dot(yan, wo_ref[:WIDTH_A, :]) + _dot(ybn, wo_ref[WIDTH_A:, :])
    o_ref[0] = x_ref[0] + _rms(y, gp_ref[...])


def _outproj(ya, ybt, x, ga, gb, wo, gp, *, tm):
    b, s, d = x.shape
    full = lambda a: pl.BlockSpec(a.shape, lambda bi, i: (0,) * a.ndim)
    return pl.pallas_call(
        _outproj_kernel,
        grid=(b, s // tm),
        in_specs=[
            pl.BlockSpec((1, tm, WIDTH_A), lambda bi, i: (bi, i, 0)),
            pl.BlockSpec((1, WIDTH_B, tm), lambda bi, i: (bi, 0, i)),
            pl.BlockSpec((1, tm, d), lambda bi, i: (bi, i, 0)),
            full(ga), full(gb), full(wo), full(gp),
        ],
        out_specs=pl.BlockSpec((1, tm, d), lambda bi, i: (bi, i, 0)),
        out_shape=jax.ShapeDtypeStruct((b, s, d), F32),
        compiler_params=pltpu.CompilerParams(
            dimension_semantics=("parallel", "parallel"), vmem_limit_bytes=VMEM_LIMIT),
        name="outproj",
    )(ya, ybt, x, ga, gb, wo, gp)


FF_GROUPS = ((0, 4), (4, 8), (8, 11))


def _ffn_kernel(taps_ref, xp_ref, x_ref, xn_ref, g_ref, wup_ref, cw_ref, cb_ref, wd_ref, gp_ref, o_ref,
                ubuf, abuf, *, tm):
    i = pl.program_id(1)
    g = g_ref[...]
    hp = _rms(xp_ref[0], g) * jnp.where(i > 0, 1.0, 0.0)
    hn = _rms(xn_ref[0], g) * jnp.where(i < pl.num_programs(1) - 1, 1.0, 0.0)
    h = jnp.concatenate([hp, _rms(x_ref[0], g), hn], axis=0).astype(BF16)
    nslab = FF_CHUNK // LANES
    nc = D_FF // FF_CHUNK

    def up(j, slot):
        for part in range(2):
            c0 = part * D_FF + j * FF_CHUNK
            u = _dot(h, wup_ref[:, c0:c0 + FF_CHUNK])
            for t in range(nslab):
                ubuf[slot, part, t] = u[:, t * LANES:(t + 1) * LANES]

    def conv(slot, part, j, t):
        c0 = part * D_FF + j * FF_CHUNK + t * LANES
        cw = cw_ref[:, c0:c0 + LANES]
        acc = cb_ref[:, c0:c0 + LANES]
        for tap in range(3):
            acc = acc + ubuf[slot, part, t, pl.ds(taps_ref[tap], tm, stride=1), :] * cw[tap:tap + 1]
        return acc

    def gate(j, slot):
        for t in range(nslab):
            a = jax.nn.gelu(conv(slot, 0, j, t), approximate=True) * conv(slot, 1, j, t)
            abuf[:, j * FF_CHUNK + t * LANES:j * FF_CHUNK + (t + 1) * LANES] = a.astype(BF16)

    up(0, 0)
    y = None
    for j in range(nc):
        if j + 1 < nc:
            up(j + 1, (j + 1) % 2)
        gate(j, j % 2)
        for lo, hi in FF_GROUPS:
            if j + 1 == hi:
                part = _dot(abuf[:, lo * FF_CHUNK:hi * FF_CHUNK], wd_ref[lo * FF_CHUNK:hi * FF_CHUNK, :])
                y = part if y is None else y + part
    o_ref[0] = x_ref[0] + _rms(y, gp_ref[...])


def _ffn(x, g, wup, cw, cb, wd, gp, *, tm):
    b, s, d = x.shape
    nt = s // tm
    hb = tm // HALO
    last = s // HALO - 1
    taps = jnp.arange(HALO - 1, HALO + 2, dtype=jnp.int32)
    resident = lambda a: pl.BlockSpec(a.shape, lambda bi, i: (0,) * a.ndim, pipeline_mode=pl.Buffered(1))
    return pl.pallas_call(
        functools.partial(_ffn_kernel, tm=tm),
        grid=(b, nt),
        in_specs=[
            pl.BlockSpec(memory_space=pltpu.SMEM),
            pl.BlockSpec((1, HALO, d), lambda bi, i: (bi, jnp.maximum(i * hb - 1, 0), 0)),
            pl.BlockSpec((1, tm, d), lambda bi, i: (bi, i, 0)),
            pl.BlockSpec((1, HALO, d), lambda bi, i: (bi, jnp.minimum((i + 1) * hb, last), 0)),
            resident(g), resident(wup), resident(cw), resident(cb), resident(wd), resident(gp),
        ],
        out_specs=pl.BlockSpec((1, tm, d), lambda bi, i: (bi, i, 0)),
        out_shape=jax.ShapeDtypeStruct((b, s, d), F32),
        scratch_shapes=[
            pltpu.VMEM((2, 2, FF_CHUNK // LANES, tm + 2 * HALO, LANES), F32),
            pltpu.VMEM((tm, D_FF), BF16),
        ],
        compiler_params=pltpu.CompilerParams(
            dimension_semantics=("parallel", "parallel"), vmem_limit_bytes=VMEM_LIMIT),
        name="convffn",
    )(taps, x, x, x, g, wup, cw, cb, wd, gp)


def _rot_cols(w):
    half = w.shape[-1] // 2
    return jnp.concatenate([-w[..., half:], w[..., :half]], axis=-1)


def _pad_cols(w, before, total):
    return jnp.pad(w, [(0, 0)] * (w.ndim - 1) + [(before, total - before - w.shape[-1])])


def _layer(x, norm_mix_pre, w_in, q_lat_norm, w_uq, kv_lat_norm, w_ukv, out_norm_a, out_norm_b, w_o,
           norm_mix_post, norm_ffn_pre, w_up, conv_w, conv_b, w_down, norm_ffn_post):
    b, s, d = x.shape
    row = lambda v: v.reshape(1, -1).astype(F32)

    c0 = 3 * WIDTH_A + Q_LORA + KV_LORA
    w_kr = w_in[:, c0:]
    win = jnp.concatenate(
        [w_in[:, :c0], _pad_cols(jnp.concatenate([w_kr, _rot_cols(w_kr)], axis=1), QK_NOPE, LANES)],
        axis=1).astype(BF16)
    wq3 = w_uq.reshape(Q_LORA, N_HEADS, QK_NOPE + QK_ROPE)
    wq = jnp.concatenate([wq3, _rot_cols(wq3[..., QK_NOPE:])], axis=-1).reshape(
        Q_LORA, N_HEADS * LANES).astype(BF16)
    wkv3 = w_ukv.reshape(KV_LORA, N_HEADS, QK_NOPE + V_DIM)
    wk = _pad_cols(wkv3[..., :QK_NOPE], 0, LANES).reshape(KV_LORA, N_HEADS * LANES).astype(BF16)
    wvt = wkv3[..., QK_NOPE:].reshape(KV_LORA, WIDTH_B).T.astype(BF16)

    pos = jnp.arange(s, dtype=F32)
    inv_freq = jnp.exp(-math.log(ROPE_BASE) * jnp.arange(0, QK_ROPE, 2, dtype=F32) / QK_ROPE)
    ang = pos[:, None] * inv_freq[None, :]
    cosp = _pad_cols(jnp.tile(jnp.cos(ang), (1, 2)), QK_NOPE, LANES)
    sinp = _pad_cols(jnp.tile(jnp.sin(ang), (1, 2)), QK_NOPE, LANES)

    q_scale = (QK_NOPE + QK_ROPE) ** -0.5 * LOG2E
    pa, q, k, vt, stats = _inproj(x, row(norm_mix_pre), win, cosp, sinp, row(q_lat_norm), wq,
                           row(kv_lat_norm), wk, wvt, tm=512, q_scale=q_scale)
    ya = _dilated_attention(pa)
    ybt = _mla_attention(q, k, vt, stats, tq=4096)
    x1 = _outproj(ya, ybt, x, row(out_norm_a), row(out_norm_b), w_o.astype(BF16), row(norm_mix_post), tm=512)
    return _ffn(x1, row(norm_ffn_pre), w_up.astype(BF16), conv_w, conv_b.reshape(1, -1),
                w_down.astype(BF16), row(norm_ffn_post), tm=512)


def kernel(x, norm_mix_pre, w_in, q_lat_norm, w_uq, kv_lat_norm, w_ukv, out_norm_a, out_norm_b, w_o,
           norm_mix_post, norm_ffn_pre, w_up, conv_w, conv_b, w_down, norm_ffn_post):
    for l in range(norm_mix_pre.shape[0]):
        x = _layer(x, norm_mix_pre[l], w_in[l], q_lat_norm[l], w_uq[l], kv_lat_norm[l], w_ukv[l],
                   out_norm_a[l], out_norm_b[l], w_o[l], norm_mix_post[l], norm_ffn_pre[l], w_up[l],
                   conv_w[l], conv_b[l], w_down[l], norm_ffn_post[l])
    return x
```

```python
import functools
import math

import numpy as np
import jax
import jax.numpy as jnp
from jax import lax
from jax.experimental import pallas as pl
from jax.experimental.pallas import tpu as pltpu

F32 = jnp.float32
BF16 = jnp.bfloat16

EPS = 1e-6
NEG = -1e30
LOG2E = 1.4426950408889634

D_MODEL = 1024
N_HEADS = 8
HEAD_DIM_A = 64
WIDTH_A = N_HEADS * HEAD_DIM_A
DILATIONS = (1, 4, 16)
HALF = 64
Q_BLOCK = 128
SLAB = Q_BLOCK + 2 * HALF
SUPER = Q_BLOCK * DILATIONS[-1]
KV_PAD = HALF * DILATIONS[-1]
QK_NOPE = 64
QK_ROPE = 32
V_DIM = 64
WIDTH_B = N_HEADS * V_DIM
Q_LORA = 384
KV_LORA = 256
ROPE_BASE = 10000.0
D_FF = 2816
FF_CHUNK = 256
LANES = 128
HALO = 16
ONES_ROWS = 16

VMEM_LIMIT = 56 * 1024 * 1024


def _rms(x, g):
    return x * lax.rsqrt(jnp.mean(x * x, axis=-1, keepdims=True) + EPS) * g


def _dot(a, b):
    return jnp.dot(a, b, preferred_element_type=F32)


def _dot_nt(a, b):
    return lax.dot_general(a, b, (((1,), (1,)), ((), ())), preferred_element_type=F32)


ROT_SHIFT = LANES - QK_ROPE
SHIFT_LANE = QK_NOPE + QK_ROPE
BOUND_SLACK = 1.02
MAX_SHIFT = 60.0


def _inproj_kernel(x_ref, g_ref, win_ref, cos_ref, sin_ref, gq_ref, wq_ref, gkv_ref, wk_ref,
                   wvt_ref, ind_ref, pa_ref, q_ref, k_ref, vt_ref, st_ref, *, q_scale):
    h = _rms(x_ref[0], g_ref[...]).astype(BF16)
    pa_ref[0] = _dot(h, win_ref[:, :3 * WIDTH_A])
    lat = _dot(h, win_ref[:, 3 * WIDTH_A:])
    c_q = lat[:, :Q_LORA]
    c_kv = lat[:, Q_LORA:Q_LORA + KV_LORA]
    kr = lat[:, Q_LORA + KV_LORA:]
    cosp = cos_ref[...]
    sinp = sin_ref[...]
    lane = lax.broadcasted_iota(jnp.int32, cosp.shape, 1)
    cq_tab = (cosp + jnp.where(lane < QK_NOPE, 1.0, 0.0)) * q_scale
    sq_tab = sinp * q_scale

    def max_sq_norm(rows):
        sq = jnp.concatenate([(v * v).astype(BF16) for v in rows], axis=1)
        return jnp.max(_dot(sq, ind_ref[...]), axis=0, keepdims=True)

    cqn = _rms(c_q, gq_ref[...]).astype(BF16)
    qf = _dot(cqn, wq_ref[...])
    qs = []
    for hh in range(N_HEADS):
        qh = qf[:, hh * LANES:(hh + 1) * LANES]
        qs.append(qh * cq_tab + pltpu.roll(qh, ROT_SHIFT, 1) * sq_tab)
        q_ref[0, hh] = qs[-1].astype(BF16)

    ckvn = _rms(c_kv, gkv_ref[...]).astype(BF16)
    kf = _dot(ckvn, wk_ref[...])
    krope = kr * cosp + pltpu.roll(kr, ROT_SHIFT, 1) * sinp + jnp.where(lane == SHIFT_LANE, 1.0, 0.0)
    ks = []
    for hh in range(N_HEADS):
        ks.append(kf[:, hh * LANES:(hh + 1) * LANES] + krope)
        k_ref[0, hh] = ks[-1].astype(BF16)
    st_ref[0, 0] = jnp.concatenate(
        [max_sq_norm(qs), max_sq_norm(ks), jnp.zeros((6, LANES), F32)], axis=0)
    vt_ref[0, 0] = _dot_nt(wvt_ref[...], ckvn).astype(BF16)


def _inproj(x, g, win, cosp, sinp, gq, wq, gkv, wk, wvt, *, tm, q_scale):
    b, s, d = x.shape
    nt = s // tm
    ind = jnp.asarray(np.repeat(np.eye(N_HEADS, LANES, dtype=np.float32), LANES, axis=0), BF16)
    full = lambda a: pl.BlockSpec(a.shape, lambda bi, i: (0,) * a.ndim)
    return pl.pallas_call(
        functools.partial(_inproj_kernel, q_scale=q_scale),
        grid=(b, nt),
        in_specs=[
            pl.BlockSpec((1, tm, d), lambda bi, i: (bi, i, 0)),
            full(g), full(win),
            pl.BlockSpec((tm, LANES), lambda bi, i: (i, 0)),
            pl.BlockSpec((tm, LANES), lambda bi, i: (i, 0)),
            full(gq), full(wq), full(gkv), full(wk), full(wvt), full(ind),
        ],
        out_specs=[
            pl.BlockSpec((1, tm, 3 * WIDTH_A), lambda bi, i: (bi, i, 0)),
            pl.BlockSpec((1, N_HEADS, tm, LANES), lambda bi, i: (bi, 0, i, 0)),
            pl.BlockSpec((1, N_HEADS, tm, LANES), lambda bi, i: (bi, 0, i, 0)),
            pl.BlockSpec((1, 1, WIDTH_B, tm), lambda bi, i: (bi, i, 0, 0)),
            pl.BlockSpec((1, 1, 8, LANES), lambda bi, i: (bi, i, 0, 0)),
        ],
        out_shape=[
            jax.ShapeDtypeStruct((b, s, 3 * WIDTH_A), F32),
            jax.ShapeDtypeStruct((b, N_HEADS, s, LANES), BF16),
            jax.ShapeDtypeStruct((b, N_HEADS, s, LANES), BF16),
            jax.ShapeDtypeStruct((b, nt, WIDTH_B, tm), BF16),
            jax.ShapeDtypeStruct((b, nt, 8, LANES), F32),
        ],
        compiler_params=pltpu.CompilerParams(
            dimension_semantics=("parallel", "parallel"), vmem_limit_bytes=VMEM_LIMIT),
        name="inproj",
    )(x, g, win, cosp, sinp, gq, wq, gkv, wk, wvt, ind)


RES_O, RES_L, RES_M = 0, 1, 2
R_FAR = DILATIONS[-1]
R_MID = 4


def _dilated_kernel(q_ref, k_ref, v_ref, tbl_ref, o_ref, kpad, vpad, tblv, tmp, kfar, vfar, res, *, seq, q_scale):
    sb = pl.program_id(2)
    t0 = sb * SUPER
    lp_far = seq // R_FAR + 2 * HALF

    @pl.when(sb == 0)
    def _():
        zeros = jnp.zeros((KV_PAD, LANES), F32)
        kpad[pl.ds(0, KV_PAD), :] = zeros
        vpad[pl.ds(0, KV_PAD), :] = zeros
        kpad[pl.ds(KV_PAD + seq, KV_PAD), :] = zeros
        vpad[pl.ds(KV_PAD + seq, KV_PAD), :] = zeros
        kpad[pl.ds(KV_PAD, seq), :] = k_ref[0]
        vpad[pl.ds(KV_PAD, seq), :] = v_ref[0]
        colk = lax.broadcasted_iota(jnp.int32, (2 * Q_BLOCK, SLAB), 1)
        for gi in range(len(DILATIONS)):
            base = tbl_ref[0, gi]
            tblv[gi, 0] = base
            tblv[gi, 1] = base + jnp.where(colk < HALF, NEG, 0.0)
            tblv[gi, 2] = base + jnp.where(colk >= SLAB - HALF, NEG, 0.0)
        sub = R_FAR // R_MID
        for src, dst in ((kpad, kfar), (vpad, vfar)):
            for c_lo in range(R_MID):
                tmp[...] = src[pl.ds(c_lo, sub * lp_far, stride=R_MID), :]
                for c_hi in range(sub):
                    dst[pl.ds((c_lo + R_MID * c_hi) * lp_far, lp_far), :] = (
                        tmp[pl.ds(c_hi, lp_far, stride=sub), :].astype(BF16))

    lane_k = lax.broadcasted_iota(jnp.int32, (SLAB, LANES), 1)
    lane_q = lax.broadcasted_iota(jnp.int32, (Q_BLOCK, LANES), 1)

    for gi, r in enumerate(DILATIONS):
        nblk_c = SUPER // (Q_BLOCK * r)
        shift = nblk_c.bit_length() - 1

        def body(i, carry, gi=gi, r=r, nblk_c=nblk_c, shift=shift):
            c = lax.shift_right_logical(i, shift)
            n = i & (nblk_c - 1)
            qstart = c + n * (Q_BLOCK * r)
            kstart = t0 + qstart - HALF * r
            qb = q_ref[0, pl.ds(qstart, Q_BLOCK, stride=r), :] * q_scale
            if r == R_FAR:
                row0 = pl.multiple_of(c * lp_far + (sb * nblk_c + n) * Q_BLOCK, Q_BLOCK)
                ks = kfar[pl.ds(row0, SLAB), :]
                vs = vfar[pl.ds(row0, SLAB), :]
            else:
                ks = kpad[pl.ds(KV_PAD + kstart, SLAB, stride=r), :].astype(BF16)
                vs = vpad[pl.ds(KV_PAD + kstart, SLAB, stride=r), :]
            q2 = jnp.concatenate([jnp.where(lane_q < HEAD_DIM_A, qb, 0.0),
                                  jnp.where(lane_q < HEAD_DIM_A, 0.0, qb)], axis=0).astype(BF16)
            variant = jnp.where(kstart < 0, 1, jnp.where(kstart + (SLAB - 1) * r >= seq, 2, 0))
            s = _dot_nt(q2, ks) + tblv[gi, variant]
            m = jnp.max(s, axis=-1, keepdims=True)
            p = jnp.exp2(s - m).astype(BF16)
            one = jnp.ones((), vs.dtype)
            rhs = jnp.concatenate([jnp.where(lane_k < HEAD_DIM_A, vs, one),
                                   jnp.where(lane_k < HEAD_DIM_A, one, vs)], axis=1).astype(BF16)
            o = _dot(p, rhs)
            rows = pl.ds(qstart, Q_BLOCK, stride=r)
            head0 = lane_q < HEAD_DIM_A
            res[gi, RES_O, rows, :] = jnp.where(head0, o[:Q_BLOCK, :LANES], o[Q_BLOCK:, LANES:])
            res[gi, RES_L, rows, :] = jnp.where(head0, o[:Q_BLOCK, LANES:], o[Q_BLOCK:, :LANES])
            res[gi, RES_M, rows, :] = jnp.where(head0, m[:Q_BLOCK], m[Q_BLOCK:])
            return carry

        lax.fori_loop(0, SUPER // Q_BLOCK, body, 0, unroll=8)

    rows = 256

    def combine(i, carry):
        sl = pl.ds(i * rows, rows)
        ms = [res[gi, RES_M, sl, :] for gi in range(len(DILATIONS))]
        mx = functools.reduce(jnp.maximum, ms)
        ws = [jnp.exp2(mm - mx) for mm in ms]
        num = sum(w * res[gi, RES_O, sl, :] for gi, w in enumerate(ws))
        den = sum(w * res[gi, RES_L, sl, :] for gi, w in enumerate(ws))
        o_ref[0, sl, :] = num / den
        return carry

    lax.fori_loop(0, SUPER // rows, combine, 0)


def _dilated_bias_table():
    off = np.abs(np.arange(SLAB)[None, :] - HALF - np.arange(Q_BLOCK)[:, None]).astype(np.float64)
    slopes = 2.0 ** (-8.0 * np.arange(1, N_HEADS + 1) / N_HEADS)
    tbl = np.empty((N_HEADS // 2, len(DILATIONS), 2, Q_BLOCK, SLAB), np.float32)
    for hp in range(N_HEADS // 2):
        for gi, r in enumerate(DILATIONS):
            for hh in range(2):
                bias = -slopes[hp * 2 + hh] * off * r * LOG2E
                tbl[hp, gi, hh] = np.where(off <= HALF, bias, NEG)
    return tbl.reshape(N_HEADS // 2, len(DILATIONS), 2 * Q_BLOCK, SLAB)


def _dilated_attention(pa):
    b, s, _ = pa.shape
    assert s % SUPER == 0 and s // DILATIONS[-1] >= 2 * Q_BLOCK
    nsb = s // SUPER
    npair = N_HEADS // 2
    tbl = jnp.asarray(_dilated_bias_table())
    q_scale = HEAD_DIM_A ** -0.5 * LOG2E
    return pl.pallas_call(
        functools.partial(_dilated_kernel, seq=s, q_scale=q_scale),
        grid=(b, npair, nsb),
        in_specs=[
            pl.BlockSpec((1, SUPER, LANES), lambda bi, hp, sb: (bi, sb, hp)),
            pl.BlockSpec((1, s, LANES), lambda bi, hp, sb: (bi, 0, npair + hp)),
            pl.BlockSpec((1, s, LANES), lambda bi, hp, sb: (bi, 0, 2 * npair + hp)),
            pl.BlockSpec((1, len(DILATIONS), 2 * Q_BLOCK, SLAB), lambda bi, hp, sb: (hp, 0, 0, 0)),
        ],
        out_specs=pl.BlockSpec((1, SUPER, LANES), lambda bi, hp, sb: (bi, sb, hp)),
        out_shape=jax.ShapeDtypeStruct((b, s, WIDTH_A), F32),
        scratch_shapes=[
            pltpu.VMEM((s + 2 * KV_PAD, LANES), F32),
            pltpu.VMEM((s + 2 * KV_PAD, LANES), F32),
            pltpu.VMEM((len(DILATIONS), 3, 2 * Q_BLOCK, SLAB), F32),
            pltpu.VMEM((s // R_MID + 2 * HALF * (R_FAR // R_MID), LANES), F32),
            pltpu.VMEM((s + 2 * HALF * R_FAR, LANES), BF16),
            pltpu.VMEM((s + 2 * HALF * R_FAR, LANES), BF16),
            pltpu.VMEM((len(DILATIONS), 3, SUPER, LANES), F32),
        ],
        compiler_params=pltpu.CompilerParams(
            dimension_semantics=("parallel", "parallel", "arbitrary"), vmem_limit_bytes=VMEM_LIMIT),
        name="dilated_attn",
    )(pa, pa, pa, tbl)


def _mla_kernel(q_ref, k_ref, vt_ref, o_ref, s_a, s_b, p_a, p_b, *, tk):
    q = q_ref[0, 0]
    tq = q.shape[0]
    nk = k_ref.shape[2] // tk

    def scores(j, s_buf):
        ks = k_ref[0, 0, pl.ds(pl.multiple_of(j * tk, tk), tk), :]
        s = _dot_nt(ks, q)
        s_buf[...] = s
        return jnp.max(s, axis=0, keepdims=True)

    def softmax(s_buf, p_buf, tmax, m):
        mn = jnp.maximum(m, tmax)
        p_buf[...] = jnp.exp2(s_buf[...] - mn).astype(BF16)
        return mn, jnp.exp2(m - mn)

    ones_rows = jnp.ones((ONES_ROWS, tk), BF16)

    def pv(j, p_buf, alpha, acc):
        return alpha * acc + _dot(jnp.concatenate([vt_ref[0, j], ones_rows], axis=0), p_buf[...])

    def trip(i, carry, last=False):
        t_a, al_b, m, acc = carry
        t_b = scores(2 * i + 1, s_b)
        m, al_a = softmax(s_a, p_a, t_a, m)
        acc = pv(jnp.maximum(2 * i - 1, 0), p_b, al_b, acc)
        if not last:
            t_a = scores(2 * i + 2, s_a)
        m, al_b = softmax(s_b, p_b, t_b, m)
        acc = pv(2 * i, p_a, al_a, acc)
        return t_a, al_b, m, acc

    p_b[...] = jnp.zeros_like(p_b)
    carry = (scores(0, s_a), jnp.ones((1, tq), F32), jnp.full((1, tq), -jnp.inf, F32),
             jnp.zeros((V_DIM + ONES_ROWS, tq), F32))
    carry = lax.fori_loop(0, nk // 2 - 1, trip, carry)
    _, al_b, _, acc = trip(nk // 2 - 1, carry, last=True)
    acc = pv(nk - 1, p_b, al_b, acc)
    o_ref[0] = acc[:V_DIM] / acc[V_DIM:V_DIM + 1]


def _mla_bound_kernel(shift_ref, q_ref, k_ref, vt_ref, o_ref, qa, p_a, p_b, *, tk):
    bh = pl.program_id(0) * pl.num_programs(1) + pl.program_id(1)
    q = q_ref[0, 0]
    tq = q.shape[0]
    nk = k_ref.shape[2] // tk
    lane = lax.broadcasted_iota(jnp.int32, q.shape, 1)
    qa[...] = jnp.where(lane == SHIFT_LANE, -shift_ref[bh], q.astype(F32)).astype(BF16)
    ones_rows = jnp.ones((ONES_ROWS, tk), BF16)

    def probs(j, p_buf):
        ks = k_ref[0, 0, pl.ds(pl.multiple_of(j * tk, tk), tk), :]
        p_buf[...] = jnp.exp2(_dot_nt(ks, qa[...])).astype(BF16)

    def pv(j, p_buf, acc):
        return acc + _dot(jnp.concatenate([vt_ref[0, j], ones_rows], axis=0), p_buf[...])

    def trip(i, acc):
        probs(2 * i, p_a)
        acc = pv(jnp.maximum(2 * i - 1, 0), p_b, acc)
        probs(2 * i + 1, p_b)
        return pv(2 * i, p_a, acc)

    p_b[...] = jnp.zeros_like(p_b)
    acc = lax.fori_loop(0, nk // 2, trip, jnp.zeros((V_DIM + ONES_ROWS, tq), F32))
    acc = pv(nk - 1, p_b, acc)
    o_ref[0] = acc[:V_DIM] / acc[V_DIM:V_DIM + 1]


def _mla_attention(q, k, vt, stats, *, tq_online, tq_bound):
    b, nh, s, _ = q.shape
    _, nkt, _, tk = vt.shape

    def specs(tq):
        return dict(
            grid=(b, nh, s // tq),
            out_specs=pl.BlockSpec((1, V_DIM, tq), lambda bi, h, i: (bi, h, i)),
            out_shape=jax.ShapeDtypeStruct((b, nh * V_DIM, s), F32),
            compiler_params=pltpu.CompilerParams(
                dimension_semantics=("parallel", "parallel", "arbitrary"), vmem_limit_bytes=VMEM_LIMIT),
        )

    def qkv_specs(tq):
        return [
            pl.BlockSpec((1, 1, tq, LANES), lambda bi, h, i: (bi, h, i, 0)),
            pl.BlockSpec((1, 1, s, LANES), lambda bi, h, i: (bi, h, 0, 0)),
            pl.BlockSpec((1, nkt, V_DIM, tk), lambda bi, h, i: (bi, 0, h, 0)),
        ]

    def online(q, k, vt, shift):
        tq = tq_online
        return pl.pallas_call(
            functools.partial(_mla_kernel, tk=tk), in_specs=qkv_specs(tq),
            scratch_shapes=[pltpu.VMEM((tk, tq), F32), pltpu.VMEM((tk, tq), F32),
                            pltpu.VMEM((tk, tq), BF16), pltpu.VMEM((tk, tq), BF16)],
            name="mla_attn", **specs(tq))(q, k, vt)

    def bounded(q, k, vt, shift):
        tq = tq_bound
        return pl.pallas_call(
            functools.partial(_mla_bound_kernel, tk=tk),
            in_specs=[pl.BlockSpec(memory_space=pltpu.SMEM)] + qkv_specs(tq),
            scratch_shapes=[pltpu.VMEM((tq, LANES), BF16), pltpu.VMEM((tk, tq), BF16),
                            pltpu.VMEM((tk, tq), BF16)],
            name="mla_attn_bound", **specs(tq))(shift.reshape(-1), q, k, vt)

    norms = jnp.sqrt(jnp.max(stats[:, :, :2, :nh], axis=1))
    shift = norms[:, 0] * norms[:, 1] * BOUND_SLACK
    return lax.cond(jnp.max(shift) <= MAX_SHIFT, bounded, online, q, k, vt, shift)


def _outproj_kernel(ya_ref, ybt_ref, x_ref, ga_ref, gb_ref, wo_ref, gp_ref, o_ref):
    yan = _rms(ya_ref[0], ga_ref[...]).astype(BF16)
    ybn = _rms(ybt_ref[0].T, gb_ref[...]).astype(BF16)
    y = _dot(yan, wo_ref[:WIDTH_A, :]) + _dot(ybn, wo_ref[WIDTH_A:, :])
    o_ref[0] = x_ref[0] + _rms(y, gp_ref[...])


def _outproj(ya, ybt, x, ga, gb, wo, gp, *, tm):
    b, s, d = x.shape
    full = lambda a: pl.BlockSpec(a.shape, lambda bi, i: (0,) * a.ndim)
    return pl.pallas_call(
        _outproj_kernel,
        grid=(b, s // tm),
        in_specs=[
            pl.BlockSpec((1, tm, WIDTH_A), lambda bi, i: (bi, i, 0)),
            pl.BlockSpec((1, WIDTH_B, tm), lambda bi, i: (bi, 0, i)),
            pl.BlockSpec((1, tm, d), lambda bi, i: (bi, i, 0)),
            full(ga), full(gb), full(wo), full(gp),
        ],
        out_specs=pl.BlockSpec((1, tm, d), lambda bi, i: (bi, i, 0)),
        out_shape=jax.ShapeDtypeStruct((b, s, d), F32),
        compiler_params=pltpu.CompilerParams(
            dimension_semantics=("parallel", "parallel"), vmem_limit_bytes=VMEM_LIMIT),
        name="outproj",
    )(ya, ybt, x, ga, gb, wo, gp)


FF_GROUPS = ((0, 4), (4, 8), (8, 11))


def _ffn_kernel(taps_ref, xp_ref, x_ref, xn_ref, g_ref, wup_ref, cw_ref, cb_ref, wd_ref, gp_ref, o_ref,
                ubuf, abuf, *, tm):
    i = pl.program_id(1)
    g = g_ref[...]
    hp = _rms(xp_ref[0], g) * jnp.where(i > 0, 1.0, 0.0)
    hn = _rms(xn_ref[0], g) * jnp.where(i < pl.num_programs(1) - 1, 1.0, 0.0)
    h = jnp.concatenate([hp, _rms(x_ref[0], g), hn], axis=0).astype(BF16)
    nslab = FF_CHUNK // LANES
    nc = D_FF // FF_CHUNK

    def up(j, slot):
        for part in range(2):
            c0 = part * D_FF + j * FF_CHUNK
            u = _dot(h, wup_ref[:, c0:c0 + FF_CHUNK])
            for t in range(nslab):
                ubuf[slot, part, t] = u[:, t * LANES:(t + 1) * LANES]

    def conv(slot, part, j, t):
        c0 = part * D_FF + j * FF_CHUNK + t * LANES
        cw = cw_ref[:, c0:c0 + LANES]
        acc = cb_ref[:, c0:c0 + LANES]
        for tap in range(3):
            acc = acc + ubuf[slot, part, t, pl.ds(taps_ref[tap], tm, stride=1), :] * cw[tap:tap + 1]
        return acc

    def gate(j, slot):
        for t in range(nslab):
            a = jax.nn.gelu(conv(slot, 0, j, t), approximate=True) * conv(slot, 1, j, t)
            abuf[:, j * FF_CHUNK + t * LANES:j * FF_CHUNK + (t + 1) * LANES] = a.astype(BF16)

    up(0, 0)
    y = None
    for j in range(nc):
        if j + 1 < nc:
            up(j + 1, (j + 1) % 2)
        gate(j, j % 2)
        for lo, hi in FF_GROUPS:
            if j + 1 == hi:
                part = _dot(abuf[:, lo * FF_CHUNK:hi * FF_CHUNK], wd_ref[lo * FF_CHUNK:hi * FF_CHUNK, :])
                y = part if y is None else y + part
    o_ref[0] = x_ref[0] + _rms(y, gp_ref[...])


def _ffn(x, g, wup, cw, cb, wd, gp, *, tm):
    b, s, d = x.shape
    nt = s // tm
    hb = tm // HALO
    last = s // HALO - 1
    taps = jnp.arange(HALO - 1, HALO + 2, dtype=jnp.int32)
    resident = lambda a: pl.BlockSpec(a.shape, lambda bi, i: (0,) * a.ndim, pipeline_mode=pl.Buffered(1))
    return pl.pallas_call(
        functools.partial(_ffn_kernel, tm=tm),
        grid=(b, nt),
        in_specs=[
            pl.BlockSpec(memory_space=pltpu.SMEM),
            pl.BlockSpec((1, HALO, d), lambda bi, i: (bi, jnp.maximum(i * hb - 1, 0), 0)),
            pl.BlockSpec((1, tm, d), lambda bi, i: (bi, i, 0)),
            pl.BlockSpec((1, HALO, d), lambda bi, i: (bi, jnp.minimum((i + 1) * hb, last), 0)),
            resident(g), resident(wup), resident(cw), resident(cb), resident(wd), resident(gp),
        ],
        out_specs=pl.BlockSpec((1, tm, d), lambda bi, i: (bi, i, 0)),
        out_shape=jax.ShapeDtypeStruct((b, s, d), F32),
        scratch_shapes=[
            pltpu.VMEM((2, 2, FF_CHUNK // LANES, tm + 2 * HALO, LANES), F32),
            pltpu.VMEM((tm, D_FF), BF16),
        ],
        compiler_params=pltpu.CompilerParams(
            dimension_semantics=("parallel", "parallel"), vmem_limit_bytes=VMEM_LIMIT),
        name="convffn",
    )(taps, x, x, x, g, wup, cw, cb, wd, gp)


def _rot_cols(w):
    half = w.shape[-1] // 2
    return jnp.concatenate([-w[..., half:], w[..., :half]], axis=-1)


def _pad_cols(w, before, total):
    return jnp.pad(w, [(0, 0)] * (w.ndim - 1) + [(before, total - before - w.shape[-1])])


def _layer(x, norm_mix_pre, w_in, q_lat_norm, w_uq, kv_lat_norm, w_ukv, out_norm_a, out_norm_b, w_o,
           norm_mix_post, norm_ffn_pre, w_up, conv_w, conv_b, w_down, norm_ffn_post):
    b, s, d = x.shape
    row = lambda v: v.reshape(1, -1).astype(F32)

    c0 = 3 * WIDTH_A + Q_LORA + KV_LORA
    w_kr = w_in[:, c0:]
    win = jnp.concatenate(
        [w_in[:, :c0], _pad_cols(jnp.concatenate([w_kr, _rot_cols(w_kr)], axis=1), QK_NOPE, LANES)],
        axis=1).astype(BF16)
    wq3 = w_uq.reshape(Q_LORA, N_HEADS, QK_NOPE + QK_ROPE)
    wq = jnp.concatenate([wq3, _rot_cols(wq3[..., QK_NOPE:])], axis=-1).reshape(
        Q_LORA, N_HEADS * LANES).astype(BF16)
    wkv3 = w_ukv.reshape(KV_LORA, N_HEADS, QK_NOPE + V_DIM)
    wk = _pad_cols(wkv3[..., :QK_NOPE], 0, LANES).reshape(KV_LORA, N_HEADS * LANES).astype(BF16)
    wvt = wkv3[..., QK_NOPE:].reshape(KV_LORA, WIDTH_B).T.astype(BF16)

    pos = jnp.arange(s, dtype=F32)
    inv_freq = jnp.exp(-math.log(ROPE_BASE) * jnp.arange(0, QK_ROPE, 2, dtype=F32) / QK_ROPE)
    ang = pos[:, None] * inv_freq[None, :]
    cosp = _pad_cols(jnp.tile(jnp.cos(ang), (1, 2)), QK_NOPE, LANES)
    sinp = _pad_cols(jnp.tile(jnp.sin(ang), (1, 2)), QK_NOPE, LANES)

    q_scale = (QK_NOPE + QK_ROPE) ** -0.5 * LOG2E
    pa, q, k, vt, stats = _inproj(x, row(norm_mix_pre), win, cosp, sinp, row(q_lat_norm), wq,
                           row(kv_lat_norm), wk, wvt, tm=512, q_scale=q_scale)
    ya = _dilated_attention(pa)
    ybt = _mla_attention(q, k, vt, stats, tq_online=min(4096, s), tq_bound=min(8192, s))
    x1 = _outproj(ya, ybt, x, row(out_norm_a), row(out_norm_b), w_o.astype(BF16), row(norm_mix_post), tm=512)
    return _ffn(x1, row(norm_ffn_pre), w_up.astype(BF16), conv_w, conv_b.reshape(1, -1),
                w_down.astype(BF16), row(norm_ffn_post), tm=512)


def kernel(x, norm_mix_pre, w_in, q_lat_norm, w_uq, kv_lat_norm, w_ukv, out_norm_a, out_norm_b, w_o,
           norm_mix_post, norm_ffn_pre, w_up, conv_w, conv_b, w_down, norm_ffn_post):
    for l in range(norm_mix_pre.shape[0]):
        x = _layer(x, norm_mix_pre[l], w_in[l], q_lat_norm[l], w_uq[l], kv_lat_norm[l], w_ukv[l],
                   out_norm_a[l], out_norm_b[l], w_o[l], norm_mix_post[l], norm_ffn_pre[l], w_up[l],
                   conv_w[l], conv_b[l], w_down[l], norm_ffn_post[l])
    return x
```

```python
import functools
import math

import numpy as np
import jax
import jax.numpy as jnp
from jax import lax
from jax.experimental import pallas as pl
from jax.experimental.pallas import tpu as pltpu

F32 = jnp.float32
BF16 = jnp.bfloat16

EPS = 1e-6
NEG = -1e30
LOG2E = 1.4426950408889634

D_MODEL = 1024
N_HEADS = 8
HEAD_DIM_A = 64
WIDTH_A = N_HEADS * HEAD_DIM_A
DILATIONS = (1, 4, 16)
HALF = 64
Q_BLOCK = 128
SLAB = Q_BLOCK + 2 * HALF
SUPER = Q_BLOCK * DILATIONS[-1]
KV_PAD = HALF * DILATIONS[-1]
QK_NOPE = 64
QK_ROPE = 32
V_DIM = 64
WIDTH_B = N_HEADS * V_DIM
Q_LORA = 384
KV_LORA = 256
ROPE_BASE = 10000.0
D_FF = 2816
FF_CHUNK = 256
LANES = 128
HALO = 16
ONES_ROWS = 16

VMEM_LIMIT = 56 * 1024 * 1024


def _rms(x, g):
    return x * lax.rsqrt(jnp.mean(x * x, axis=-1, keepdims=True) + EPS) * g


def _dot(a, b):
    return jnp.dot(a, b, preferred_element_type=F32)


def _dot_nt(a, b):
    return lax.dot_general(a, b, (((1,), (1,)), ((), ())), preferred_element_type=F32)


ROT_SHIFT = LANES - QK_ROPE
SHIFT_LANE = QK_NOPE + QK_ROPE
BOUND_SLACK = 1.02
MAX_SHIFT = 60.0


def _inproj_kernel(x_ref, g_ref, win_ref, cos_ref, sin_ref, gq_ref, wq_ref, gkv_ref, wk_ref,
                   wvt_ref, ind_ref, pa_ref, q_ref, k_ref, vt_ref, st_ref, *, q_scale):
    h = _rms(x_ref[0], g_ref[...]).astype(BF16)
    pa_ref[0] = _dot(h, win_ref[:, :3 * WIDTH_A])
    lat = _dot(h, win_ref[:, 3 * WIDTH_A:])
    c_q = lat[:, :Q_LORA]
    c_kv = lat[:, Q_LORA:Q_LORA + KV_LORA]
    kr = lat[:, Q_LORA + KV_LORA:]
    cosp = cos_ref[...]
    sinp = sin_ref[...]
    lane = lax.broadcasted_iota(jnp.int32, cosp.shape, 1)
    cq_tab = (cosp + jnp.where(lane < QK_NOPE, 1.0, 0.0)) * q_scale
    sq_tab = sinp * q_scale

    def max_sq_norm(rows):
        sq = jnp.concatenate([(v * v).astype(BF16) for v in rows], axis=1)
        return jnp.max(_dot(sq, ind_ref[...]), axis=0, keepdims=True)

    cqn = _rms(c_q, gq_ref[...]).astype(BF16)
    qf = _dot(cqn, wq_ref[...])
    qs = []
    for hh in range(N_HEADS):
        qh = qf[:, hh * LANES:(hh + 1) * LANES]
        qs.append(qh * cq_tab + pltpu.roll(qh, ROT_SHIFT, 1) * sq_tab)
        q_ref[0, hh] = qs[-1].astype(BF16)

    ckvn = _rms(c_kv, gkv_ref[...]).astype(BF16)
    kf = _dot(ckvn, wk_ref[...])
    krope = kr * cosp + pltpu.roll(kr, ROT_SHIFT, 1) * sinp + jnp.where(lane == SHIFT_LANE, 1.0, 0.0)
    ks = []
    for hh in range(N_HEADS):
        ks.append(kf[:, hh * LANES:(hh + 1) * LANES] + krope)
        k_ref[0, hh] = ks[-1].astype(BF16)
    st_ref[0, 0] = jnp.concatenate(
        [max_sq_norm(qs), max_sq_norm(ks), jnp.zeros((6, LANES), F32)], axis=0)
    vt_ref[0, 0] = _dot_nt(wvt_ref[...], ckvn).astype(BF16)


def _inproj(x, g, win, cosp, sinp, gq, wq, gkv, wk, wvt, *, tm, q_scale):
    b, s, d = x.shape
    nt = s // tm
    ind = jnp.asarray(np.repeat(np.eye(N_HEADS, LANES, dtype=np.float32), LANES, axis=0), BF16)
    full = lambda a: pl.BlockSpec(a.shape, lambda bi, i: (0,) * a.ndim)
    return pl.pallas_call(
        functools.partial(_inproj_kernel, q_scale=q_scale),
        grid=(b, nt),
        in_specs=[
            pl.BlockSpec((1, tm, d), lambda bi, i: (bi, i, 0)),
            full(g), full(win),
            pl.BlockSpec((tm, LANES), lambda bi, i: (i, 0)),
            pl.BlockSpec((tm, LANES), lambda bi, i: (i, 0)),
            full(gq), full(wq), full(gkv), full(wk), full(wvt), full(ind),
        ],
        out_specs=[
            pl.BlockSpec((1, tm, 3 * WIDTH_A), lambda bi, i: (bi, i, 0)),
            pl.BlockSpec((1, N_HEADS, tm, LANES), lambda bi, i: (bi, 0, i, 0)),
            pl.BlockSpec((1, N_HEADS, tm, LANES), lambda bi, i: (bi, 0, i, 0)),
            pl.BlockSpec((1, 1, WIDTH_B, tm), lambda bi, i: (bi, i, 0, 0)),
            pl.BlockSpec((1, 1, 8, LANES), lambda bi, i: (bi, i, 0, 0)),
        ],
        out_shape=[
            jax.ShapeDtypeStruct((b, s, 3 * WIDTH_A), F32),
            jax.ShapeDtypeStruct((b, N_HEADS, s, LANES), BF16),
            jax.ShapeDtypeStruct((b, N_HEADS, s, LANES), BF16),
            jax.ShapeDtypeStruct((b, nt, WIDTH_B, tm), BF16),
            jax.ShapeDtypeStruct((b, nt, 8, LANES), F32),
        ],
        compiler_params=pltpu.CompilerParams(
            dimension_semantics=("parallel", "parallel"), vmem_limit_bytes=VMEM_LIMIT),
        name="inproj",
    )(x, g, win, cosp, sinp, gq, wq, gkv, wk, wvt, ind)


RES_O, RES_L, RES_M = 0, 1, 2
R_FAR = DILATIONS[-1]
R_MID = 4


def _dilated_kernel(q_ref, k_ref, v_ref, tbl_ref, o_ref, kpad, vpad, tblv, tmp, kfar, vfar, res, *, seq, q_scale):
    sb = pl.program_id(2)
    t0 = sb * SUPER
    lp_far = seq // R_FAR + 2 * HALF

    @pl.when(sb == 0)
    def _():
        zeros = jnp.zeros((KV_PAD, LANES), F32)
        kpad[pl.ds(0, KV_PAD), :] = zeros
        vpad[pl.ds(0, KV_PAD), :] = zeros
        kpad[pl.ds(KV_PAD + seq, KV_PAD), :] = zeros
        vpad[pl.ds(KV_PAD + seq, KV_PAD), :] = zeros
        kpad[pl.ds(KV_PAD, seq), :] = k_ref[0]
        vpad[pl.ds(KV_PAD, seq), :] = v_ref[0]
        colk = lax.broadcasted_iota(jnp.int32, (2 * Q_BLOCK, SLAB), 1)
        for gi in range(len(DILATIONS)):
            base = tbl_ref[0, gi]
            tblv[gi, 0] = base
            tblv[gi, 1] = base + jnp.where(colk < HALF, NEG, 0.0)
            tblv[gi, 2] = base + jnp.where(colk >= SLAB - HALF, NEG, 0.0)
        sub = R_FAR // R_MID
        for src, dst in ((kpad, kfar), (vpad, vfar)):
            for c_lo in range(R_MID):
                tmp[...] = src[pl.ds(c_lo, sub * lp_far, stride=R_MID), :]
                for c_hi in range(sub):
                    dst[pl.ds((c_lo + R_MID * c_hi) * lp_far, lp_far), :] = (
                        tmp[pl.ds(c_hi, lp_far, stride=sub), :].astype(BF16))

    lane_k = lax.broadcasted_iota(jnp.int32, (SLAB, LANES), 1)
    lane_q = lax.broadcasted_iota(jnp.int32, (Q_BLOCK, LANES), 1)

    for gi, r in enumerate(DILATIONS):
        nblk_c = SUPER // (Q_BLOCK * r)
        shift = nblk_c.bit_length() - 1

        def body(i, carry, gi=gi, r=r, nblk_c=nblk_c, shift=shift):
            c = lax.shift_right_logical(i, shift)
            n = i & (nblk_c - 1)
            qstart = c + n * (Q_BLOCK * r)
            kstart = t0 + qstart - HALF * r
            qb = q_ref[0, pl.ds(qstart, Q_BLOCK, stride=r), :] * q_scale
            if r == R_FAR:
                row0 = pl.multiple_of(c * lp_far + (sb * nblk_c + n) * Q_BLOCK, Q_BLOCK)
                ks = kfar[pl.ds(row0, SLAB), :]
                vs = vfar[pl.ds(row0, SLAB), :]
            else:
                ks = kpad[pl.ds(KV_PAD + kstart, SLAB, stride=r), :].astype(BF16)
                vs = vpad[pl.ds(KV_PAD + kstart, SLAB, stride=r), :]
            q2 = jnp.concatenate([jnp.where(lane_q < HEAD_DIM_A, qb, 0.0),
                                  jnp.where(lane_q < HEAD_DIM_A, 0.0, qb)], axis=0).astype(BF16)
            variant = jnp.where(kstart < 0, 1, jnp.where(kstart + (SLAB - 1) * r >= seq, 2, 0))
            s = _dot_nt(q2, ks) + tblv[gi, variant]
            m = jnp.max(s, axis=-1, keepdims=True)
            p = jnp.exp2(s - m).astype(BF16)
            one = jnp.ones((), vs.dtype)
            rhs = jnp.concatenate([jnp.where(lane_k < HEAD_DIM_A, vs, one),
                                   jnp.where(lane_k < HEAD_DIM_A, one, vs)], axis=1).astype(BF16)
            o = _dot(p, rhs)
            rows = pl.ds(qstart, Q_BLOCK, stride=r)
            head0 = lane_q < HEAD_DIM_A
            res[gi, RES_O, rows, :] = jnp.where(head0, o[:Q_BLOCK, :LANES], o[Q_BLOCK:, LANES:])
            res[gi, RES_L, rows, :] = jnp.where(head0, o[:Q_BLOCK, LANES:], o[Q_BLOCK:, :LANES])
            res[gi, RES_M, rows, :] = jnp.where(head0, m[:Q_BLOCK], m[Q_BLOCK:])
            return carry

        lax.fori_loop(0, SUPER // Q_BLOCK, body, 0, unroll=True)

    rows = 256

    def combine(i, carry):
        sl = pl.ds(i * rows, rows)
        ms = [res[gi, RES_M, sl, :] for gi in range(len(DILATIONS))]
        mx = functools.reduce(jnp.maximum, ms)
        ws = [jnp.exp2(mm - mx) for mm in ms]
        num = sum(w * res[gi, RES_O, sl, :] for gi, w in enumerate(ws))
        den = sum(w * res[gi, RES_L, sl, :] for gi, w in enumerate(ws))
        o_ref[0, sl, :] = num / den
        return carry

    lax.fori_loop(0, SUPER // rows, combine, 0)


def _dilated_bias_table():
    off = np.abs(np.arange(SLAB)[None, :] - HALF - np.arange(Q_BLOCK)[:, None]).astype(np.float64)
    slopes = 2.0 ** (-8.0 * np.arange(1, N_HEADS + 1) / N_HEADS)
    tbl = np.empty((N_HEADS // 2, len(DILATIONS), 2, Q_BLOCK, SLAB), np.float32)
    for hp in range(N_HEADS // 2):
        for gi, r in enumerate(DILATIONS):
            for hh in range(2):
                bias = -slopes[hp * 2 + hh] * off * r * LOG2E
                tbl[hp, gi, hh] = np.where(off <= HALF, bias, NEG)
    return tbl.reshape(N_HEADS // 2, len(DILATIONS), 2 * Q_BLOCK, SLAB)


def _dilated_attention(pa):
    b, s, _ = pa.shape
    assert s % SUPER == 0 and s // DILATIONS[-1] >= 2 * Q_BLOCK
    nsb = s // SUPER
    npair = N_HEADS // 2
    tbl = jnp.asarray(_dilated_bias_table())
    q_scale = HEAD_DIM_A ** -0.5 * LOG2E
    return pl.pallas_call(
        functools.partial(_dilated_kernel, seq=s, q_scale=q_scale),
        grid=(b, npair, nsb),
        in_specs=[
            pl.BlockSpec((1, SUPER, LANES), lambda bi, hp, sb: (bi, sb, hp)),
            pl.BlockSpec((1, s, LANES), lambda bi, hp, sb: (bi, 0, npair + hp)),
            pl.BlockSpec((1, s, LANES), lambda bi, hp, sb: (bi, 0, 2 * npair + hp)),
            pl.BlockSpec((1, len(DILATIONS), 2 * Q_BLOCK, SLAB), lambda bi, hp, sb: (hp, 0, 0, 0)),
        ],
        out_specs=pl.BlockSpec((1, SUPER, LANES), lambda bi, hp, sb: (bi, sb, hp)),
        out_shape=jax.ShapeDtypeStruct((b, s, WIDTH_A), F32),
        scratch_shapes=[
            pltpu.VMEM((s + 2 * KV_PAD, LANES), F32),
            pltpu.VMEM((s + 2 * KV_PAD, LANES), F32),
            pltpu.VMEM((len(DILATIONS), 3, 2 * Q_BLOCK, SLAB), F32),
            pltpu.VMEM((s // R_MID + 2 * HALF * (R_FAR // R_MID), LANES), F32),
            pltpu.VMEM((s + 2 * HALF * R_FAR, LANES), BF16),
            pltpu.VMEM((s + 2 * HALF * R_FAR, LANES), BF16),
            pltpu.VMEM((len(DILATIONS), 3, SUPER, LANES), F32),
        ],
        compiler_params=pltpu.CompilerParams(
            dimension_semantics=("parallel", "parallel", "arbitrary"), vmem_limit_bytes=VMEM_LIMIT),
        name="dilated_attn",
    )(pa, pa, pa, tbl)


def _mla_kernel(q_ref, k_ref, vt_ref, o_ref, s_a, s_b, p_a, p_b, *, tk):
    q = q_ref[0, 0]
    tq = q.shape[0]
    nk = k_ref.shape[2] // tk

    def scores(j, s_buf):
        ks = k_ref[0, 0, pl.ds(pl.multiple_of(j * tk, tk), tk), :]
        s = _dot_nt(ks, q)
        s_buf[...] = s
        return jnp.max(s, axis=0, keepdims=True)

    def softmax(s_buf, p_buf, tmax, m):
        mn = jnp.maximum(m, tmax)
        p_buf[...] = jnp.exp2(s_buf[...] - mn).astype(BF16)
        return mn, jnp.exp2(m - mn)

    ones_rows = jnp.ones((ONES_ROWS, tk), BF16)

    def pv(j, p_buf, alpha, acc):
        return alpha * acc + _dot(jnp.concatenate([vt_ref[0, j], ones_rows], axis=0), p_buf[...])

    def trip(i, carry, last=False):
        t_a, al_b, m, acc = carry
        t_b = scores(2 * i + 1, s_b)
        m, al_a = softmax(s_a, p_a, t_a, m)
        acc = pv(jnp.maximum(2 * i - 1, 0), p_b, al_b, acc)
        if not last:
            t_a = scores(2 * i + 2, s_a)
        m, al_b = softmax(s_b, p_b, t_b, m)
        acc = pv(2 * i, p_a, al_a, acc)
        return t_a, al_b, m, acc

    p_b[...] = jnp.zeros_like(p_b)
    carry = (scores(0, s_a), jnp.ones((1, tq), F32), jnp.full((1, tq), -jnp.inf, F32),
             jnp.zeros((V_DIM + ONES_ROWS, tq), F32))
    carry = lax.fori_loop(0, nk // 2 - 1, trip, carry)
    _, al_b, _, acc = trip(nk // 2 - 1, carry, last=True)
    acc = pv(nk - 1, p_b, al_b, acc)
    o_ref[0] = acc[:V_DIM] / acc[V_DIM:V_DIM + 1]


def _mla_bound_kernel(shift_ref, q_ref, k_ref, vt_ref, o_ref, qa, p_a, p_b, *, tk):
    bh = pl.program_id(0) * pl.num_programs(1) + pl.program_id(1)
    q = q_ref[0, 0]
    tq = q.shape[0]
    nk = k_ref.shape[2] // tk
    lane = lax.broadcasted_iota(jnp.int32, q.shape, 1)
    qa[...] = jnp.where(lane == SHIFT_LANE, -shift_ref[bh], q.astype(F32)).astype(BF16)
    ones_rows = jnp.ones((ONES_ROWS, tk), BF16)

    def probs(j, p_buf):
        ks = k_ref[0, 0, pl.ds(pl.multiple_of(j * tk, tk), tk), :]
        p_buf[...] = jnp.exp2(_dot_nt(ks, qa[...])).astype(BF16)

    def pv(j, p_buf, acc):
        return acc + _dot(jnp.concatenate([vt_ref[0, j], ones_rows], axis=0), p_buf[...])

    def trip(i, acc):
        probs(2 * i, p_a)
        acc = pv(jnp.maximum(2 * i - 1, 0), p_b, acc)
        probs(2 * i + 1, p_b)
        return pv(2 * i, p_a, acc)

    p_b[...] = jnp.zeros_like(p_b)
    acc = lax.fori_loop(0, nk // 2, trip, jnp.zeros((V_DIM + ONES_ROWS, tq), F32))
    acc = pv(nk - 1, p_b, acc)
    o_ref[0] = acc[:V_DIM] / acc[V_DIM:V_DIM + 1]


def _mla_attention(q, k, vt, stats, *, tq_online, tq_bound):
    b, nh, s, _ = q.shape
    _, nkt, _, tk = vt.shape

    def specs(tq):
        return dict(
            grid=(b, nh, s // tq),
            out_specs=pl.BlockSpec((1, V_DIM, tq), lambda bi, h, i: (bi, h, i)),
            out_shape=jax.ShapeDtypeStruct((b, nh * V_DIM, s), F32),
            compiler_params=pltpu.CompilerParams(
                dimension_semantics=("parallel", "parallel", "arbitrary"), vmem_limit_bytes=VMEM_LIMIT),
        )

    def qkv_specs(tq):
        return [
            pl.BlockSpec((1, 1, tq, LANES), lambda bi, h, i: (bi, h, i, 0)),
            pl.BlockSpec((1, 1, s, LANES), lambda bi, h, i: (bi, h, 0, 0)),
            pl.BlockSpec((1, nkt, V_DIM, tk), lambda bi, h, i: (bi, 0, h, 0)),
        ]

    def online(q, k, vt, shift):
        tq = tq_online
        return pl.pallas_call(
            functools.partial(_mla_kernel, tk=tk), in_specs=qkv_specs(tq),
            scratch_shapes=[pltpu.VMEM((tk, tq), F32), pltpu.VMEM((tk, tq), F32),
                            pltpu.VMEM((tk, tq), BF16), pltpu.VMEM((tk, tq), BF16)],
            name="mla_attn", **specs(tq))(q, k, vt)

    def bounded(q, k, vt, shift):
        tq = tq_bound
        return pl.pallas_call(
            functools.partial(_mla_bound_kernel, tk=tk),
            in_specs=[pl.BlockSpec(memory_space=pltpu.SMEM)] + qkv_specs(tq),
            scratch_shapes=[pltpu.VMEM((tq, LANES), BF16), pltpu.VMEM((tk, tq), BF16),
                            pltpu.VMEM((tk, tq), BF16)],
            name="mla_attn_bound", **specs(tq))(shift.reshape(-1), q, k, vt)

    norms = jnp.sqrt(jnp.max(stats[:, :, :2, :nh], axis=1))
    shift = norms[:, 0] * norms[:, 1] * BOUND_SLACK
    return lax.cond(jnp.max(shift) <= MAX_SHIFT, bounded, online, q, k, vt, shift)


def _outproj_kernel(ya_ref, ybt_ref, x_ref, ga_ref, gb_ref, wo_ref, gp_ref, o_ref):
    yan = _rms(ya_ref[0], ga_ref[...]).astype(BF16)
    ybn = _rms(ybt_ref[0].T, gb_ref[...]).astype(BF16)
    y = _dot(yan, wo_ref[:WIDTH_A, :]) + _dot(ybn, wo_ref[WIDTH_A:, :])
    o_ref[0] = x_ref[0] + _rms(y, gp_ref[...])


def _outproj(ya, ybt, x, ga, gb, wo, gp, *, tm):
    b, s, d = x.shape
    full = lambda a: pl.BlockSpec(a.shape, lambda bi, i: (0,) * a.ndim)
    return pl.pallas_call(
        _outproj_kernel,
        grid=(b, s // tm),
        in_specs=[
            pl.BlockSpec((1, tm, WIDTH_A), lambda bi, i: (bi, i, 0)),
            pl.BlockSpec((1, WIDTH_B, tm), lambda bi, i: (bi, 0, i)),
            pl.BlockSpec((1, tm, d), lambda bi, i: (bi, i, 0)),
            full(ga), full(gb), full(wo), full(gp),
        ],
        out_specs=pl.BlockSpec((1, tm, d), lambda bi, i: (bi, i, 0)),
        out_shape=jax.ShapeDtypeStruct((b, s, d), F32),
        compiler_params=pltpu.CompilerParams(
            dimension_semantics=("parallel", "parallel"), vmem_limit_bytes=VMEM_LIMIT),
        name="outproj",
    )(ya, ybt, x, ga, gb, wo, gp)


FF_GROUPS = ((0, 4), (4, 8), (8, 11))


def _ffn_kernel(taps_ref, xp_ref, x_ref, xn_ref, g_ref, wup_ref, cw_ref, cb_ref, wd_ref, gp_ref, o_ref,
                ubuf, abuf, *, tm):
    i = pl.program_id(1)
    g = g_ref[...]
    hp = _rms(xp_ref[0], g) * jnp.where(i > 0, 1.0, 0.0)
    hn = _rms(xn_ref[0], g) * jnp.where(i < pl.num_programs(1) - 1, 1.0, 0.0)
    h = jnp.concatenate([hp, _rms(x_ref[0], g), hn], axis=0).astype(BF16)
    nslab = FF_CHUNK // LANES
    nc = D_FF // FF_CHUNK

    def up(j, slot):
        for part in range(2):
            c0 = part * D_FF + j * FF_CHUNK
            u = _dot(h, wup_ref[:, c0:c0 + FF_CHUNK])
            for t in range(nslab):
                ubuf[slot, part, t] = u[:, t * LANES:(t + 1) * LANES]

    def conv(slot, part, j, t):
        c0 = part * D_FF + j * FF_CHUNK + t * LANES
        cw = cw_ref[:, c0:c0 + LANES]
        acc = cb_ref[:, c0:c0 + LANES]
        for tap in range(3):
            acc = acc + ubuf[slot, part, t, pl.ds(taps_ref[tap], tm, stride=1), :] * cw[tap:tap + 1]
        return acc

    def gate(j, slot):
        for t in range(nslab):
            a = jax.nn.gelu(conv(slot, 0, j, t), approximate=True) * conv(slot, 1, j, t)
            abuf[:, j * FF_CHUNK + t * LANES:j * FF_CHUNK + (t + 1) * LANES] = a.astype(BF16)

    up(0, 0)
    y = None
    for j in range(nc):
        if j + 1 < nc:
            up(j + 1, (j + 1) % 2)
        gate(j, j % 2)
        for lo, hi in FF_GROUPS:
            if j + 1 == hi:
                part = _dot(abuf[:, lo * FF_CHUNK:hi * FF_CHUNK], wd_ref[lo * FF_CHUNK:hi * FF_CHUNK, :])
                y = part if y is None else y + part
    o_ref[0] = x_ref[0] + _rms(y, gp_ref[...])


def _ffn(x, g, wup, cw, cb, wd, gp, *, tm):
    b, s, d = x.shape
    nt = s // tm
    hb = tm // HALO
    last = s // HALO - 1
    taps = jnp.arange(HALO - 1, HALO + 2, dtype=jnp.int32)
    resident = lambda a: pl.BlockSpec(a.shape, lambda bi, i: (0,) * a.ndim, pipeline_mode=pl.Buffered(1))
    return pl.pallas_call(
        functools.partial(_ffn_kernel, tm=tm),
        grid=(b, nt),
        in_specs=[
            pl.BlockSpec(memory_space=pltpu.SMEM),
            pl.BlockSpec((1, HALO, d), lambda bi, i: (bi, jnp.maximum(i * hb - 1, 0), 0)),
            pl.BlockSpec((1, tm, d), lambda bi, i: (bi, i, 0)),
            pl.BlockSpec((1, HALO, d), lambda bi, i: (bi, jnp.minimum((i + 1) * hb, last), 0)),
            resident(g), resident(wup), resident(cw), resident(cb), resident(wd), resident(gp),
        ],
        out_specs=pl.BlockSpec((1, tm, d), lambda bi, i: (bi, i, 0)),
        out_shape=jax.ShapeDtypeStruct((b, s, d), F32),
        scratch_shapes=[
            pltpu.VMEM((2, 2, FF_CHUNK // LANES, tm + 2 * HALO, LANES), F32),
            pltpu.VMEM((tm, D_FF), BF16),
        ],
        compiler_params=pltpu.CompilerParams(
            dimension_semantics=("parallel", "parallel"), vmem_limit_bytes=VMEM_LIMIT),
        name="convffn",
    )(taps, x, x, x, g, wup, cw, cb, wd, gp)


def _rot_cols(w):
    half = w.shape[-1] // 2
    return jnp.concatenate([-w[..., half:], w[..., :half]], axis=-1)


def _pad_cols(w, before, total):
    return jnp.pad(w, [(0, 0)] * (w.ndim - 1) + [(before, total - before - w.shape[-1])])


def _layer(x, norm_mix_pre, w_in, q_lat_norm, w_uq, kv_lat_norm, w_ukv, out_norm_a, out_norm_b, w_o,
           norm_mix_post, norm_ffn_pre, w_up, conv_w, conv_b, w_down, norm_ffn_post):
    b, s, d = x.shape
    row = lambda v: v.reshape(1, -1).astype(F32)

    c0 = 3 * WIDTH_A + Q_LORA + KV_LORA
    w_kr = w_in[:, c0:]
    win = jnp.concatenate(
        [w_in[:, :c0], _pad_cols(jnp.concatenate([w_kr, _rot_cols(w_kr)], axis=1), QK_NOPE, LANES)],
        axis=1).astype(BF16)
    wq3 = w_uq.reshape(Q_LORA, N_HEADS, QK_NOPE + QK_ROPE)
    wq = jnp.concatenate([wq3, _rot_cols(wq3[..., QK_NOPE:])], axis=-1).reshape(
        Q_LORA, N_HEADS * LANES).astype(BF16)
    wkv3 = w_ukv.reshape(KV_LORA, N_HEADS, QK_NOPE + V_DIM)
    wk = _pad_cols(wkv3[..., :QK_NOPE], 0, LANES).reshape(KV_LORA, N_HEADS * LANES).astype(BF16)
    wvt = wkv3[..., QK_NOPE:].reshape(KV_LORA, WIDTH_B).T.astype(BF16)

    pos = jnp.arange(s, dtype=F32)
    inv_freq = jnp.exp(-math.log(ROPE_BASE) * jnp.arange(0, QK_ROPE, 2, dtype=F32) / QK_ROPE)
    ang = pos[:, None] * inv_freq[None, :]
    cosp = _pad_cols(jnp.tile(jnp.cos(ang), (1, 2)), QK_NOPE, LANES)
    sinp = _pad_cols(jnp.tile(jnp.sin(ang), (1, 2)), QK_NOPE, LANES)

    q_scale = (QK_NOPE + QK_ROPE) ** -0.5 * LOG2E
    pa, q, k, vt, stats = _inproj(x, row(norm_mix_pre), win, cosp, sinp, row(q_lat_norm), wq,
                           row(kv_lat_norm), wk, wvt, tm=512, q_scale=q_scale)
    ya = _dilated_attention(pa)
    ybt = _mla_attention(q, k, vt, stats, tq_online=min(4096, s), tq_bound=min(8192, s))
    x1 = _outproj(ya, ybt, x, row(out_norm_a), row(out_norm_b), w_o.astype(BF16), row(norm_mix_post), tm=512)
    return _ffn(x1, row(norm_ffn_pre), w_up.astype(BF16), conv_w, conv_b.reshape(1, -1),
                w_down.astype(BF16), row(norm_ffn_post), tm=512)


def kernel(x, norm_mix_pre, w_in, q_lat_norm, w_uq, kv_lat_norm, w_ukv, out_norm_a, out_norm_b, w_o,
           norm_mix_post, norm_ffn_pre, w_up, conv_w, conv_b, w_down, norm_ffn_post):
    for l in range(norm_mix_pre.shape[0]):
        x = _layer(x, norm_mix_pre[l], w_in[l], q_lat_norm[l], w_uq[l], kv_lat_norm[l], w_ukv[l],
                   out_norm_a[l], out_norm_b[l], w_o[l], norm_mix_post[l], norm_ffn_pre[l], w_up[l],
                   conv_w[l], conv_b[l], w_down[l], norm_ffn_post[l])
    return x
```

```python
import functools
import math

import numpy as np
import jax
import jax.numpy as jnp
from jax import lax
from jax.experimental import pallas as pl
from jax.experimental.pallas import tpu as pltpu

F32 = jnp.float32
BF16 = jnp.bfloat16

EPS = 1e-6
NEG = -1e30
LOG2E = 1.4426950408889634

D_MODEL = 1024
N_HEADS = 8
HEAD_DIM_A = 64
WIDTH_A = N_HEADS * HEAD_DIM_A
DILATIONS = (1, 4, 16)
HALF = 64
Q_BLOCK = 128
SLAB = Q_BLOCK + 2 * HALF
SUPER = Q_BLOCK * DILATIONS[-1]
KV_PAD = HALF * DILATIONS[-1]
QK_NOPE = 64
QK_ROPE = 32
V_DIM = 64
WIDTH_B = N_HEADS * V_DIM
Q_LORA = 384
KV_LORA = 256
ROPE_BASE = 10000.0
D_FF = 2816
FF_CHUNK = 256
LANES = 128
HALO = 16
ONES_ROWS = 16

VMEM_LIMIT = 56 * 1024 * 1024


def _rms(x, g):
    return x * lax.rsqrt(jnp.mean(x * x, axis=-1, keepdims=True) + EPS) * g


def _dot(a, b):
    return jnp.dot(a, b, preferred_element_type=F32)


def _dot_nt(a, b):
    return lax.dot_general(a, b, (((1,), (1,)), ((), ())), preferred_element_type=F32)


ROT_SHIFT = LANES - QK_ROPE
SHIFT_LANE = QK_NOPE + QK_ROPE
BOUND_SLACK = 1.02
MAX_SHIFT = 60.0


def _inproj_kernel(x_ref, g_ref, win_ref, cos_ref, sin_ref, gq_ref, wq_ref, gkv_ref, wk_ref,
                   wvt_ref, ind_ref, pa_ref, q_ref, k_ref, vt_ref, st_ref, *, q_scale):
    h = _rms(x_ref[0], g_ref[...]).astype(BF16)
    pa_ref[0] = _dot(h, win_ref[:, :3 * WIDTH_A])
    lat = _dot(h, win_ref[:, 3 * WIDTH_A:])
    c_q = lat[:, :Q_LORA]
    c_kv = lat[:, Q_LORA:Q_LORA + KV_LORA]
    kr = lat[:, Q_LORA + KV_LORA:]
    cosp = cos_ref[...]
    sinp = sin_ref[...]
    lane = lax.broadcasted_iota(jnp.int32, cosp.shape, 1)
    cq_tab = (cosp + jnp.where(lane < QK_NOPE, 1.0, 0.0)) * q_scale
    sq_tab = sinp * q_scale

    def max_sq_norm(rows):
        sq = jnp.concatenate([(v * v).astype(BF16) for v in rows], axis=1)
        return jnp.max(_dot(sq, ind_ref[...]), axis=0, keepdims=True)

    cqn = _rms(c_q, gq_ref[...]).astype(BF16)
    qf = _dot(cqn, wq_ref[...])
    qs = []
    for hh in range(N_HEADS):
        qh = qf[:, hh * LANES:(hh + 1) * LANES]
        qs.append(qh * cq_tab + pltpu.roll(qh, ROT_SHIFT, 1) * sq_tab)
        q_ref[0, hh] = qs[-1].astype(BF16)

    ckvn = _rms(c_kv, gkv_ref[...]).astype(BF16)
    kf = _dot(ckvn, wk_ref[...])
    krope = kr * cosp + pltpu.roll(kr, ROT_SHIFT, 1) * sinp + jnp.where(lane == SHIFT_LANE, 1.0, 0.0)
    ks = []
    for hh in range(N_HEADS):
        ks.append(kf[:, hh * LANES:(hh + 1) * LANES] + krope)
        k_ref[0, hh] = ks[-1].astype(BF16)
    st_ref[0, 0] = jnp.concatenate(
        [max_sq_norm(qs), max_sq_norm(ks), jnp.zeros((6, LANES), F32)], axis=0)
    vt_ref[0, 0] = _dot_nt(wvt_ref[...], ckvn).astype(BF16)


def _inproj(x, g, win, cosp, sinp, gq, wq, gkv, wk, wvt, *, tm, q_scale):
    b, s, d = x.shape
    nt = s // tm
    ind = jnp.asarray(np.repeat(np.eye(N_HEADS, LANES, dtype=np.float32), LANES, axis=0), BF16)
    full = lambda a: pl.BlockSpec(a.shape, lambda bi, i: (0,) * a.ndim)
    return pl.pallas_call(
        functools.partial(_inproj_kernel, q_scale=q_scale),
        grid=(b, nt),
        in_specs=[
            pl.BlockSpec((1, tm, d), lambda bi, i: (bi, i, 0)),
            full(g), full(win),
            pl.BlockSpec((tm, LANES), lambda bi, i: (i, 0)),
            pl.BlockSpec((tm, LANES), lambda bi, i: (i, 0)),
            full(gq), full(wq), full(gkv), full(wk), full(wvt), full(ind),
        ],
        out_specs=[
            pl.BlockSpec((1, tm, 3 * WIDTH_A), lambda bi, i: (bi, i, 0)),
            pl.BlockSpec((1, N_HEADS, tm, LANES), lambda bi, i: (bi, 0, i, 0)),
            pl.BlockSpec((1, N_HEADS, tm, LANES), lambda bi, i: (bi, 0, i, 0)),
            pl.BlockSpec((1, 1, WIDTH_B, tm), lambda bi, i: (bi, i, 0, 0)),
            pl.BlockSpec((1, 1, 8, LANES), lambda bi, i: (bi, i, 0, 0)),
        ],
        out_shape=[
            jax.ShapeDtypeStruct((b, s, 3 * WIDTH_A), F32),
            jax.ShapeDtypeStruct((b, N_HEADS, s, LANES), BF16),
            jax.ShapeDtypeStruct((b, N_HEADS, s, LANES), BF16),
            jax.ShapeDtypeStruct((b, nt, WIDTH_B, tm), BF16),
            jax.ShapeDtypeStruct((b, nt, 8, LANES), F32),
        ],
        compiler_params=pltpu.CompilerParams(
            dimension_semantics=("parallel", "parallel"), vmem_limit_bytes=VMEM_LIMIT),
        name="inproj",
    )(x, g, win, cosp, sinp, gq, wq, gkv, wk, wvt, ind)


RES_O, RES_L, RES_M = 0, 1, 2
R_FAR = DILATIONS[-1]
R_MID = 4


def _dilated_kernel(q_ref, k_ref, v_ref, tbl_ref, o_ref, kpad, vpad, tblv, tmp, kfar, vfar, res, *, seq, q_scale):
    sb = pl.program_id(2)
    t0 = sb * SUPER
    lp_far = seq // R_FAR + 2 * HALF

    @pl.when(sb == 0)
    def _():
        zeros = jnp.zeros((KV_PAD, LANES), F32)
        kpad[pl.ds(0, KV_PAD), :] = zeros
        vpad[pl.ds(0, KV_PAD), :] = zeros
        kpad[pl.ds(KV_PAD + seq, KV_PAD), :] = zeros
        vpad[pl.ds(KV_PAD + seq, KV_PAD), :] = zeros
        kpad[pl.ds(KV_PAD, seq), :] = k_ref[0]
        vpad[pl.ds(KV_PAD, seq), :] = v_ref[0]
        colk = lax.broadcasted_iota(jnp.int32, (2 * Q_BLOCK, SLAB), 1)
        for gi in range(len(DILATIONS)):
            base = tbl_ref[0, gi]
            tblv[gi, 0] = base
            tblv[gi, 1] = base + jnp.where(colk < HALF, NEG, 0.0)
            tblv[gi, 2] = base + jnp.where(colk >= SLAB - HALF, NEG, 0.0)
        sub = R_FAR // R_MID
        for src, dst in ((kpad, kfar), (vpad, vfar)):
            for c_lo in range(R_MID):
                tmp[...] = src[pl.ds(c_lo, sub * lp_far, stride=R_MID), :]
                for c_hi in range(sub):
                    dst[pl.ds((c_lo + R_MID * c_hi) * lp_far, lp_far), :] = (
                        tmp[pl.ds(c_hi, lp_far, stride=sub), :].astype(BF16))

    lane_k = lax.broadcasted_iota(jnp.int32, (SLAB, LANES), 1)
    lane_q = lax.broadcasted_iota(jnp.int32, (Q_BLOCK, LANES), 1)

    for gi, r in enumerate(DILATIONS):
        nblk_c = SUPER // (Q_BLOCK * r)
        shift = nblk_c.bit_length() - 1

        def body(i, carry, gi=gi, r=r, nblk_c=nblk_c, shift=shift):
            c = lax.shift_right_logical(i, shift)
            n = i & (nblk_c - 1)
            qstart = c + n * (Q_BLOCK * r)
            kstart = t0 + qstart - HALF * r
            qb = q_ref[0, pl.ds(qstart, Q_BLOCK, stride=r), :] * q_scale
            if r == R_FAR:
                row0 = pl.multiple_of(c * lp_far + (sb * nblk_c + n) * Q_BLOCK, Q_BLOCK)
                ks = kfar[pl.ds(row0, SLAB), :]
                vs = vfar[pl.ds(row0, SLAB), :]
            else:
                ks = kpad[pl.ds(KV_PAD + kstart, SLAB, stride=r), :].astype(BF16)
                vs = vpad[pl.ds(KV_PAD + kstart, SLAB, stride=r), :]
            q2 = jnp.concatenate([jnp.where(lane_q < HEAD_DIM_A, qb, 0.0),
                                  jnp.where(lane_q < HEAD_DIM_A, 0.0, qb)], axis=0).astype(BF16)
            variant = jnp.where(kstart < 0, 1, jnp.where(kstart + (SLAB - 1) * r >= seq, 2, 0))
            s = _dot_nt(q2, ks) + tblv[gi, variant]
            m = jnp.max(s, axis=-1, keepdims=True)
            p = jnp.exp2(s - m).astype(BF16)
            one = jnp.ones((), vs.dtype)
            rhs = jnp.concatenate([jnp.where(lane_k < HEAD_DIM_A, vs, one),
                                   jnp.where(lane_k < HEAD_DIM_A, one, vs)], axis=1).astype(BF16)
            o = _dot(p, rhs)
            rows = pl.ds(qstart, Q_BLOCK, stride=r)
            head0 = lane_q < HEAD_DIM_A
            res[gi, RES_O, rows, :] = jnp.where(head0, o[:Q_BLOCK, :LANES], o[Q_BLOCK:, LANES:])
            res[gi, RES_L, rows, :] = jnp.where(head0, o[:Q_BLOCK, LANES:], o[Q_BLOCK:, :LANES])
            res[gi, RES_M, rows, :] = jnp.where(head0, m[:Q_BLOCK], m[Q_BLOCK:])
            return carry

        lax.fori_loop(0, SUPER // Q_BLOCK, body, 0, unroll=True)

    rows = 256

    def combine(i, carry):
        sl = pl.ds(i * rows, rows)
        ms = [res[gi, RES_M, sl, :] for gi in range(len(DILATIONS))]
        mx = functools.reduce(jnp.maximum, ms)
        ws = [jnp.exp2(mm - mx) for mm in ms]
        num = sum(w * res[gi, RES_O, sl, :] for gi, w in enumerate(ws))
        den = sum(w * res[gi, RES_L, sl, :] for gi, w in enumerate(ws))
        o_ref[0, sl, :] = num / den
        return carry

    lax.fori_loop(0, SUPER // rows, combine, 0)


def _dilated_bias_table():
    off = np.abs(np.arange(SLAB)[None, :] - HALF - np.arange(Q_BLOCK)[:, None]).astype(np.float64)
    slopes = 2.0 ** (-8.0 * np.arange(1, N_HEADS + 1) / N_HEADS)
    tbl = np.empty((N_HEADS // 2, len(DILATIONS), 2, Q_BLOCK, SLAB), np.float32)
    for hp in range(N_HEADS // 2):
        for gi, r in enumerate(DILATIONS):
            for hh in range(2):
                bias = -slopes[hp * 2 + hh] * off * r * LOG2E
                tbl[hp, gi, hh] = np.where(off <= HALF, bias, NEG)
    return tbl.reshape(N_HEADS // 2, len(DILATIONS), 2 * Q_BLOCK, SLAB)


def _dilated_attention(pa):
    b, s, _ = pa.shape
    assert s % SUPER == 0 and s // DILATIONS[-1] >= 2 * Q_BLOCK
    nsb = s // SUPER
    npair = N_HEADS // 2
    tbl = jnp.asarray(_dilated_bias_table())
    q_scale = HEAD_DIM_A ** -0.5 * LOG2E
    return pl.pallas_call(
        functools.partial(_dilated_kernel, seq=s, q_scale=q_scale),
        grid=(b, npair, nsb),
        in_specs=[
            pl.BlockSpec((1, SUPER, LANES), lambda bi, hp, sb: (bi, sb, hp)),
            pl.BlockSpec((1, s, LANES), lambda bi, hp, sb: (bi, 0, npair + hp)),
            pl.BlockSpec((1, s, LANES), lambda bi, hp, sb: (bi, 0, 2 * npair + hp)),
            pl.BlockSpec((1, len(DILATIONS), 2 * Q_BLOCK, SLAB), lambda bi, hp, sb: (hp, 0, 0, 0)),
        ],
        out_specs=pl.BlockSpec((1, SUPER, LANES), lambda bi, hp, sb: (bi, sb, hp)),
        out_shape=jax.ShapeDtypeStruct((b, s, WIDTH_A), F32),
        scratch_shapes=[
            pltpu.VMEM((s + 2 * KV_PAD, LANES), F32),
            pltpu.VMEM((s + 2 * KV_PAD, LANES), F32),
            pltpu.VMEM((len(DILATIONS), 3, 2 * Q_BLOCK, SLAB), F32),
            pltpu.VMEM((s // R_MID + 2 * HALF * (R_FAR // R_MID), LANES), F32),
            pltpu.VMEM((s + 2 * HALF * R_FAR, LANES), BF16),
            pltpu.VMEM((s + 2 * HALF * R_FAR, LANES), BF16),
            pltpu.VMEM((len(DILATIONS), 3, SUPER, LANES), F32),
        ],
        compiler_params=pltpu.CompilerParams(
            dimension_semantics=("parallel", "parallel", "arbitrary"), vmem_limit_bytes=VMEM_LIMIT),
        name="dilated_attn",
    )(pa, pa, pa, tbl)


def _mla_kernel(q_ref, k_ref, vt_ref, o_ref, s_a, s_b, p_a, p_b, *, tk):
    q = q_ref[0, 0]
    tq = q.shape[0]
    nk = k_ref.shape[2] // tk

    def scores(j, s_buf):
        ks = k_ref[0, 0, pl.ds(pl.multiple_of(j * tk, tk), tk), :]
        s = _dot_nt(ks, q)
        s_buf[...] = s
        return jnp.max(s, axis=0, keepdims=True)

    def softmax(s_buf, p_buf, tmax, m):
        mn = jnp.maximum(m, tmax)
        p_buf[...] = jnp.exp2(s_buf[...] - mn).astype(BF16)
        return mn, jnp.exp2(m - mn)

    ones_rows = jnp.ones((ONES_ROWS, tk), BF16)

    def pv(j, p_buf, alpha, acc):
        return alpha * acc + _dot(jnp.concatenate([vt_ref[0, j], ones_rows], axis=0), p_buf[...])

    def trip(i, carry, last=False):
        t_a, al_b, m, acc = carry
        t_b = scores(2 * i + 1, s_b)
        m, al_a = softmax(s_a, p_a, t_a, m)
        acc = pv(jnp.maximum(2 * i - 1, 0), p_b, al_b, acc)
        if not last:
            t_a = scores(2 * i + 2, s_a)
        m, al_b = softmax(s_b, p_b, t_b, m)
        acc = pv(2 * i, p_a, al_a, acc)
        return t_a, al_b, m, acc

    p_b[...] = jnp.zeros_like(p_b)
    carry = (scores(0, s_a), jnp.ones((1, tq), F32), jnp.full((1, tq), -jnp.inf, F32),
             jnp.zeros((V_DIM + ONES_ROWS, tq), F32))
    carry = lax.fori_loop(0, nk // 2 - 1, trip, carry)
    _, al_b, _, acc = trip(nk // 2 - 1, carry, last=True)
    acc = pv(nk - 1, p_b, al_b, acc)
    o_ref[0] = acc[:V_DIM] / acc[V_DIM:V_DIM + 1]


def _mla_bound_kernel(shift_ref, q_ref, k_ref, vt_ref, o_ref, qa, p_a, p_b, *, tk):
    bh = pl.program_id(0) * pl.num_programs(1) + pl.program_id(1)
    q = q_ref[0, 0]
    tq = q.shape[0]
    nk = k_ref.shape[2] // tk
    lane = lax.broadcasted_iota(jnp.int32, q.shape, 1)
    qa[...] = jnp.where(lane == SHIFT_LANE, -shift_ref[bh], q.astype(F32)).astype(BF16)
    ones_rows = jnp.ones((ONES_ROWS, tk), BF16)

    def probs(j, p_buf):
        ks = k_ref[0, 0, pl.ds(pl.multiple_of(j * tk, tk), tk), :]
        p_buf[...] = jnp.exp2(_dot_nt(ks, qa[...])).astype(BF16)

    def pv(j, p_buf, acc):
        return acc + _dot(jnp.concatenate([vt_ref[0, j], ones_rows], axis=0), p_buf[...])

    def trip(i, acc):
        probs(2 * i, p_a)
        acc = pv(jnp.maximum(2 * i - 1, 0), p_b, acc)
        probs(2 * i + 1, p_b)
        return pv(2 * i, p_a, acc)

    p_b[...] = jnp.zeros_like(p_b)
    acc = lax.fori_loop(0, nk // 2, trip, jnp.zeros((V_DIM + ONES_ROWS, tq), F32), unroll=2)
    acc = pv(nk - 1, p_b, acc)
    o_ref[0] = acc[:V_DIM] / acc[V_DIM:V_DIM + 1]


def _mla_attention(q, k, vt, stats, *, tq_online, tq_bound):
    b, nh, s, _ = q.shape
    _, nkt, _, tk = vt.shape

    def specs(tq):
        return dict(
            grid=(b, nh, s // tq),
            out_specs=pl.BlockSpec((1, V_DIM, tq), lambda bi, h, i: (bi, h, i)),
            out_shape=jax.ShapeDtypeStruct((b, nh * V_DIM, s), F32),
            compiler_params=pltpu.CompilerParams(
                dimension_semantics=("parallel", "parallel", "arbitrary"), vmem_limit_bytes=VMEM_LIMIT),
        )

    def qkv_specs(tq):
        return [
            pl.BlockSpec((1, 1, tq, LANES), lambda bi, h, i: (bi, h, i, 0)),
            pl.BlockSpec((1, 1, s, LANES), lambda bi, h, i: (bi, h, 0, 0)),
            pl.BlockSpec((1, nkt, V_DIM, tk), lambda bi, h, i: (bi, 0, h, 0)),
        ]

    def online(q, k, vt, shift):
        tq = tq_online
        return pl.pallas_call(
            functools.partial(_mla_kernel, tk=tk), in_specs=qkv_specs(tq),
            scratch_shapes=[pltpu.VMEM((tk, tq), F32), pltpu.VMEM((tk, tq), F32),
                            pltpu.VMEM((tk, tq), BF16), pltpu.VMEM((tk, tq), BF16)],
            name="mla_attn", **specs(tq))(q, k, vt)

    def bounded(q, k, vt, shift):
        tq = tq_bound
        return pl.pallas_call(
            functools.partial(_mla_bound_kernel, tk=tk),
            in_specs=[pl.BlockSpec(memory_space=pltpu.SMEM)] + qkv_specs(tq),
            scratch_shapes=[pltpu.VMEM((tq, LANES), BF16), pltpu.VMEM((tk, tq), BF16),
                            pltpu.VMEM((tk, tq), BF16)],
            name="mla_attn_bound", **specs(tq))(shift.reshape(-1), q, k, vt)

    norms = jnp.sqrt(jnp.max(stats[:, :, :2, :nh], axis=1))
    shift = norms[:, 0] * norms[:, 1] * BOUND_SLACK
    return lax.cond(jnp.max(shift) <= MAX_SHIFT, bounded, online, q, k, vt, shift)


def _outproj_kernel(ya_ref, ybt_ref, x_ref, ga_ref, gb_ref, wo_ref, gp_ref, o_ref):
    yan = _rms(ya_ref[0], ga_ref[...]).astype(BF16)
    ybn = _rms(ybt_ref[0].T, gb_ref[...]).astype(BF16)
    y = _dot(yan, wo_ref[:WIDTH_A, :]) + _dot(ybn, wo_ref[WIDTH_A:, :])
    o_ref[0] = x_ref[0] + _rms(y, gp_ref[...])


def _outproj(ya, ybt, x, ga, gb, wo, gp, *, tm):
    b, s, d = x.shape
    full = lambda a: pl.BlockSpec(a.shape, lambda bi, i: (0,) * a.ndim)
    return pl.pallas_call(
        _outproj_kernel,
        grid=(b, s // tm),
        in_specs=[
            pl.BlockSpec((1, tm, WIDTH_A), lambda bi, i: (bi, i, 0)),
            pl.BlockSpec((1, WIDTH_B, tm), lambda bi, i: (bi, 0, i)),
            pl.BlockSpec((1, tm, d), lambda bi, i: (bi, i, 0)),
            full(ga), full(gb), full(wo), full(gp),
        ],
        out_specs=pl.BlockSpec((1, tm, d), lambda bi, i: (bi, i, 0)),
        out_shape=jax.ShapeDtypeStruct((b, s, d), F32),
        compiler_params=pltpu.CompilerParams(
            dimension_semantics=("parallel", "parallel"), vmem_limit_bytes=VMEM_LIMIT),
        name="outproj",
    )(ya, ybt, x, ga, gb, wo, gp)


FF_GROUPS = ((0, 4), (4, 8), (8, 11))


def _ffn_kernel(taps_ref, xp_ref, x_ref, xn_ref, g_ref, wup_ref, cw_ref, cb_ref, wd_ref, gp_ref, o_ref,
                ubuf, abuf, *, tm):
    i = pl.program_id(1)
    g = g_ref[...]
    hp = _rms(xp_ref[0], g) * jnp.where(i > 0, 1.0, 0.0)
    hn = _rms(xn_ref[0], g) * jnp.where(i < pl.num_programs(1) - 1, 1.0, 0.0)
    h = jnp.concatenate([hp, _rms(x_ref[0], g), hn], axis=0).astype(BF16)
    nslab = FF_CHUNK // LANES
    nc = D_FF // FF_CHUNK

    def up(j, slot):
        for part in range(2):
            c0 = part * D_FF + j * FF_CHUNK
            u = _dot(h, wup_ref[:, c0:c0 + FF_CHUNK])
            for t in range(nslab):
                ubuf[slot, part, t] = u[:, t * LANES:(t + 1) * LANES]

    def conv(slot, part, j, t):
        c0 = part * D_FF + j * FF_CHUNK + t * LANES
        cw = cw_ref[:, c0:c0 + LANES]
        acc = cb_ref[:, c0:c0 + LANES]
        for tap in range(3):
            acc = acc + ubuf[slot, part, t, pl.ds(taps_ref[tap], tm, stride=1), :] * cw[tap:tap + 1]
        return acc

    def gate(j, slot):
        for t in range(nslab):
            a = jax.nn.gelu(conv(slot, 0, j, t), approximate=True) * conv(slot, 1, j, t)
            abuf[:, j * FF_CHUNK + t * LANES:j * FF_CHUNK + (t + 1) * LANES] = a.astype(BF16)

    up(0, 0)
    y = None
    for j in range(nc):
        if j + 1 < nc:
            up(j + 1, (j + 1) % 2)
        gate(j, j % 2)
        for lo, hi in FF_GROUPS:
            if j + 1 == hi:
                part = _dot(abuf[:, lo * FF_CHUNK:hi * FF_CHUNK], wd_ref[lo * FF_CHUNK:hi * FF_CHUNK, :])
                y = part if y is None else y + part
    o_ref[0] = x_ref[0] + _rms(y, gp_ref[...])


def _ffn(x, g, wup, cw, cb, wd, gp, *, tm):
    b, s, d = x.shape
    nt = s // tm
    hb = tm // HALO
    last = s // HALO - 1
    taps = jnp.arange(HALO - 1, HALO + 2, dtype=jnp.int32)
    resident = lambda a: pl.BlockSpec(a.shape, lambda bi, i: (0,) * a.ndim, pipeline_mode=pl.Buffered(1))
    return pl.pallas_call(
        functools.partial(_ffn_kernel, tm=tm),
        grid=(b, nt),
        in_specs=[
            pl.BlockSpec(memory_space=pltpu.SMEM),
            pl.BlockSpec((1, HALO, d), lambda bi, i: (bi, jnp.maximum(i * hb - 1, 0), 0)),
            pl.BlockSpec((1, tm, d), lambda bi, i: (bi, i, 0)),
            pl.BlockSpec((1, HALO, d), lambda bi, i: (bi, jnp.minimum((i + 1) * hb, last), 0)),
            resident(g), resident(wup), resident(cw), resident(cb), resident(wd), resident(gp),
        ],
        out_specs=pl.BlockSpec((1, tm, d), lambda bi, i: (bi, i, 0)),
        out_shape=jax.ShapeDtypeStruct((b, s, d), F32),
        scratch_shapes=[
            pltpu.VMEM((2, 2, FF_CHUNK // LANES, tm + 2 * HALO, LANES), F32),
            pltpu.VMEM((tm, D_FF), BF16),
        ],
        compiler_params=pltpu.CompilerParams(
            dimension_semantics=("parallel", "parallel"), vmem_limit_bytes=VMEM_LIMIT),
        name="convffn",
    )(taps, x, x, x, g, wup, cw, cb, wd, gp)


def _rot_cols(w):
    half = w.shape[-1] // 2
    return jnp.concatenate([-w[..., half:], w[..., :half]], axis=-1)


def _pad_cols(w, before, total):
    return jnp.pad(w, [(0, 0)] * (w.ndim - 1) + [(before, total - before - w.shape[-1])])


def _layer(x, norm_mix_pre, w_in, q_lat_norm, w_uq, kv_lat_norm, w_ukv, out_norm_a, out_norm_b, w_o,
           norm_mix_post, norm_ffn_pre, w_up, conv_w, conv_b, w_down, norm_ffn_post):
    b, s, d = x.shape
    row = lambda v: v.reshape(1, -1).astype(F32)

    c0 = 3 * WIDTH_A + Q_LORA + KV_LORA
    w_kr = w_in[:, c0:]
    win = jnp.concatenate(
        [w_in[:, :c0], _pad_cols(jnp.concatenate([w_kr, _rot_cols(w_kr)], axis=1), QK_NOPE, LANES)],
        axis=1).astype(BF16)
    wq3 = w_uq.reshape(Q_LORA, N_HEADS, QK_NOPE + QK_ROPE)
    wq = jnp.concatenate([wq3, _rot_cols(wq3[..., QK_NOPE:])], axis=-1).reshape(
        Q_LORA, N_HEADS * LANES).astype(BF16)
    wkv3 = w_ukv.reshape(KV_LORA, N_HEADS, QK_NOPE + V_DIM)
    wk = _pad_cols(wkv3[..., :QK_NOPE], 0, LANES).reshape(KV_LORA, N_HEADS * LANES).astype(BF16)
    wvt = wkv3[..., QK_NOPE:].reshape(KV_LORA, WIDTH_B).T.astype(BF16)

    pos = jnp.arange(s, dtype=F32)
    inv_freq = jnp.exp(-math.log(ROPE_BASE) * jnp.arange(0, QK_ROPE, 2, dtype=F32) / QK_ROPE)
    ang = pos[:, None] * inv_freq[None, :]
    cosp = _pad_cols(jnp.tile(jnp.cos(ang), (1, 2)), QK_NOPE, LANES)
    sinp = _pad_cols(jnp.tile(jnp.sin(ang), (1, 2)), QK_NOPE, LANES)

    q_scale = (QK_NOPE + QK_ROPE) ** -0.5 * LOG2E
    pa, q, k, vt, stats = _inproj(x, row(norm_mix_pre), win, cosp, sinp, row(q_lat_norm), wq,
                           row(kv_lat_norm), wk, wvt, tm=512, q_scale=q_scale)
    ya = _dilated_attention(pa)
    ybt = _mla_attention(q, k, vt, stats, tq_online=min(4096, s), tq_bound=min(8192, s))
    x1 = _outproj(ya, ybt, x, row(out_norm_a), row(out_norm_b), w_o.astype(BF16), row(norm_mix_post), tm=512)
    return _ffn(x1, row(norm_ffn_pre), w_up.astype(BF16), conv_w, conv_b.reshape(1, -1),
                w_down.astype(BF16), row(norm_ffn_post), tm=512)


def kernel(x, norm_mix_pre, w_in, q_lat_norm, w_uq, kv_lat_norm, w_ukv, out_norm_a, out_norm_b, w_o,
           norm_mix_post, norm_ffn_pre, w_up, conv_w, conv_b, w_down, norm_ffn_post):
    for l in range(norm_mix_pre.shape[0]):
        x = _layer(x, norm_mix_pre[l], w_in[l], q_lat_norm[l], w_uq[l], kv_lat_norm[l], w_ukv[l],
                   out_norm_a[l], out_norm_b[l], w_o[l], norm_mix_post[l], norm_ffn_pre[l], w_up[l],
                   conv_w[l], conv_b[l], w_down[l], norm_ffn_post[l])
    return x
```
